```python
import jax, jax.numpy as jnp
from jax import lax
import numpy as np

D_MODEL = 1024
BATCH = 1
SEQ = 16384
DEPTH = 1

PLE_DIM = 256
GRID_W = 64
ATTN_HEAD_DIM = 64
ATTN_HEADS = (D_MODEL // 2) // ATTN_HEAD_DIM
ATTN_KV_HEADS = ATTN_HEADS // 4
RET_HEAD_DIM = 128
RET_HEADS = (D_MODEL // 2) // RET_HEAD_DIM
ATTN_Q_W = ATTN_HEADS * ATTN_HEAD_DIM
ATTN_KV_W = ATTN_KV_HEADS * ATTN_HEAD_DIM
RET_W = RET_HEADS * RET_HEAD_DIM
IN_SPLITS = (ATTN_Q_W, ATTN_KV_W, ATTN_KV_W, RET_W, RET_W, RET_W, RET_W, D_MODEL, D_MODEL)
IN_W = ATTN_Q_W + 2 * ATTN_KV_W + 4 * RET_W + 2 * D_MODEL
D_FF = 4 * D_MODEL
Q_BLOCK = 128
RET_CHUNK = 128
ROPE_THETA = 10000.0
NORM_EPS = 1e-6
GN_EPS = 1e-5

kernel_name = "hybrid_gqa_retention_gated_encoder"


def rms_norm(x, gain):
    x32 = x.astype(jnp.float32)
    y = x32 * lax.rsqrt(jnp.mean(x32 * x32, axis=-1, keepdims=True) + NORM_EPS)
    return (y * gain.astype(jnp.float32)).astype(x.dtype)


def axial_rope_tables(seq_len, head_dim):
    rows = seq_len // GRID_W
    n_axis = head_dim // 4
    freqs = ROPE_THETA ** (-jnp.arange(n_axis, dtype=jnp.float32) / n_axis)
    row = jnp.repeat(jnp.arange(rows, dtype=jnp.float32), GRID_W)
    col = jnp.tile(jnp.arange(GRID_W, dtype=jnp.float32), rows)
    ang = jnp.concatenate([row[:, None] * freqs, col[:, None] * freqs], axis=-1)
    return jnp.cos(ang), jnp.sin(ang)


def apply_rope(x, cos, sin):
    x32 = x.astype(jnp.float32)
    x1, x2 = jnp.split(x32, 2, axis=-1)
    c = cos[None, :, None, :]
    s = sin[None, :, None, :]
    return jnp.concatenate([x1 * c - x2 * s, x1 * s + x2 * c], axis=-1).astype(x.dtype)


def gqa_attention(q, k, v):
    b, s, _, hd = q.shape
    g = ATTN_HEADS // ATTN_KV_HEADS
    nb = s // Q_BLOCK
    qb = q.reshape(b, nb, Q_BLOCK, ATTN_KV_HEADS, g, hd).transpose(1, 0, 3, 4, 2, 5)
    kt = k.transpose(0, 2, 1, 3)
    vt = v.transpose(0, 2, 1, 3)
    scale = hd ** -0.5

    def block(qi):
        sc = jnp.einsum('bkgqd,bksd->bkgqs', qi, kt).astype(jnp.float32) * scale
        pr = jax.nn.softmax(sc, axis=-1).astype(vt.dtype)
        return jnp.einsum('bkgqs,bksd->bkgqd', pr, vt)

    o = lax.map(block, qb)
    return o.transpose(1, 0, 4, 2, 3, 5).reshape(b, s, ATTN_HEADS * hd)


def retention_direction(q, k, v, log_gamma, include_diag):
    b, h, s, dk = q.shape
    dv = v.shape[-1]
    c = RET_CHUNK
    nc = s // c
    qc = q.reshape(b, h, nc, c, dk)
    kc = k.reshape(b, h, nc, c, dk)
    vc = v.reshape(b, h, nc, c, dv)
    pos = jnp.arange(c, dtype=jnp.float32)
    lg = log_gamma[:, None]
    diff = pos[:, None] - pos[None, :]
    mask = (diff >= 0) if include_diag else (diff > 0)
    dmask = jnp.where(mask[None], jnp.exp(lg[:, :, None] * jnp.maximum(diff, 0.0)[None]), 0.0)
    scores = jnp.einsum('bhnid,bhnjd->bhnij', qc, kc) * dmask[None, :, None]
    y = jnp.einsum('bhnij,bhnje->bhnie', scores, vc)
    k_dec = jnp.exp(lg * (c - 1 - pos))
    kv = jnp.einsum('bhnjd,bhnje->nbhde', kc * k_dec[None, :, None, :, None], vc)
    chunk_decay = jnp.exp(log_gamma * c)[None, :, None, None]

    def step(state, kv_n):
        return state * chunk_decay + kv_n, state

    _, prev = lax.scan(step, jnp.zeros((b, h, dk, dv), jnp.float32), kv)
    q_dec = jnp.exp(lg * (pos + 1.0))
    y = y + jnp.einsum('bhnid,nbhde->bhnie', qc * q_dec[None, :, None, :, None], prev)
    return y.reshape(b, h, s, dv)


def bidirectional_retention(q, k, v, decay_logit):
    log_g = jax.nn.log_sigmoid(decay_logit.astype(jnp.float32))
    qt = q.astype(jnp.float32).transpose(0, 2, 1, 3) * (RET_HEAD_DIM ** -0.5)
    kt = k.astype(jnp.float32).transpose(0, 2, 1, 3)
    vt = v.astype(jnp.float32).transpose(0, 2, 1, 3)
    fwd = retention_direction(qt, kt, vt, log_g[0], True)
    flip = lambda a: jnp.flip(a, axis=2)
    bwd = flip(retention_direction(flip(qt), flip(kt), flip(vt), log_g[1], False))
    return fwd + bwd


def hybrid_layer(x, p_i, mix_norm, w_in, attn_q_norm, attn_k_norm, ret_decay_logit, ret_norm_gain,
                 w_attn_o, w_ret_o, w_out, mlp_norm, w_up, w_down, ple_norm, w_ple_gate, w_ple,
                 rope_a, rope_r):
    b, s, _ = x.shape
    dt = x.dtype
    h = rms_norm(x, mix_norm)
    proj = h @ w_in
    split_idx = [int(i) for i in np.cumsum(IN_SPLITS)[:-1]]
    aq, ak, av, rq, rk, rv, rg, gate_a, gate_r = jnp.split(proj, split_idx, axis=-1)

    aq = rms_norm(aq.reshape(b, s, ATTN_HEADS, ATTN_HEAD_DIM), attn_q_norm)
    ak = rms_norm(ak.reshape(b, s, ATTN_KV_HEADS, ATTN_HEAD_DIM), attn_k_norm)
    av = av.reshape(b, s, ATTN_KV_HEADS, ATTN_HEAD_DIM)
    aq = apply_rope(aq, *rope_a)
    ak = apply_rope(ak, *rope_a)
    attn_out = gqa_attention(aq, ak, av).astype(dt) @ w_attn_o

    rq = apply_rope(rq.reshape(b, s, RET_HEADS, RET_HEAD_DIM), *rope_r)
    rk = apply_rope(rk.reshape(b, s, RET_HEADS, RET_HEAD_DIM), *rope_r)
    rv = rv.reshape(b, s, RET_HEADS, RET_HEAD_DIM)
    ry = bidirectional_retention(rq, rk, rv, ret_decay_logit)
    mu = jnp.mean(ry, axis=-1, keepdims=True)
    var = jnp.mean(jnp.square(ry - mu), axis=-1, keepdims=True)
    ry = ((ry - mu) * lax.rsqrt(var + GN_EPS)).transpose(0, 2, 1, 3).reshape(b, s, RET_W)
    ry = ry * ret_norm_gain.astype(jnp.float32) * jax.nn.silu(rg.astype(jnp.float32))
    ret_out = ry.astype(dt) @ w_ret_o

    merged = jax.nn.sigmoid(gate_a) * attn_out + jax.nn.sigmoid(gate_r) * ret_out
    x = x + merged @ w_out

    hm = rms_norm(x, mlp_norm)
    x = x + jnp.square(jax.nn.relu(hm @ w_up)) @ w_down

    gate = jax.nn.sigmoid(rms_norm(x, ple_norm) @ w_ple_gate)
    x = x + gate * (p_i.astype(dt) @ w_ple)
    return x


def setup_inputs(seed: int = 0) -> dict:
    key = jax.random.key(seed)
    ks = jax.random.split(key, 20)
    f32 = jnp.float32

    def w(k, shape, fan_in):
        return jax.random.normal(k, shape, f32) * (fan_in ** -0.5)

    def gain(k, shape):
        return 1.0 + 0.05 * jax.random.normal(k, shape, f32)

    gamma0 = 1.0 - 2.0 ** (-5.0 - jnp.arange(RET_HEADS, dtype=f32))
    logit0 = jnp.log(gamma0) - jnp.log1p(-gamma0)
    decay_logit = logit0[None, None, :] + 0.05 * jax.random.normal(ks[5], (DEPTH, 2, RET_HEADS), f32)

    return {
        "x": jax.random.normal(ks[0], (BATCH, SEQ, D_MODEL), f32),
        "p": jax.random.normal(ks[1], (DEPTH, BATCH, SEQ, PLE_DIM), f32),
        "mix_norm": gain(ks[2], (DEPTH, D_MODEL)),
        "w_in": w(ks[3], (DEPTH, D_MODEL, IN_W), D_MODEL),
        "attn_q_norm": gain(ks[4], (DEPTH, ATTN_HEAD_DIM)),
        "attn_k_norm": gain(ks[6], (DEPTH, ATTN_HEAD_DIM)),
        "ret_decay_logit": decay_logit,
        "ret_norm_gain": gain(ks[7], (DEPTH, RET_W)),
        "w_attn_o": w(ks[8], (DEPTH, ATTN_Q_W, D_MODEL), ATTN_Q_W),
        "w_ret_o": w(ks[9], (DEPTH, RET_W, D_MODEL), RET_W),
        "w_out": w(ks[10], (DEPTH, D_MODEL, D_MODEL), D_MODEL),
        "mlp_norm": gain(ks[11], (DEPTH, D_MODEL)),
        "w_up": w(ks[12], (DEPTH, D_MODEL, D_FF), D_MODEL),
        "w_down": w(ks[13], (DEPTH, D_FF, D_MODEL), D_FF),
        "ple_norm": gain(ks[14], (DEPTH, D_MODEL)),
        "w_ple_gate": w(ks[15], (DEPTH, D_MODEL, D_MODEL), D_MODEL),
        "w_ple": w(ks[16], (DEPTH, PLE_DIM, D_MODEL), PLE_DIM),
        "final_norm": gain(ks[17], (D_MODEL,)),
    }


def reference(x, p, mix_norm, w_in, attn_q_norm, attn_k_norm, ret_decay_logit, ret_norm_gain,
              w_attn_o, w_ret_o, w_out, mlp_norm, w_up, w_down, ple_norm, w_ple_gate, w_ple,
              final_norm):
    seq_len = x.shape[1]
    rope_a = axial_rope_tables(seq_len, ATTN_HEAD_DIM)
    rope_r = axial_rope_tables(seq_len, RET_HEAD_DIM)
    for i in range(DEPTH):
        x = hybrid_layer(x, p[i], mix_norm[i], w_in[i], attn_q_norm[i], attn_k_norm[i],
                         ret_decay_logit[i], ret_norm_gain[i], w_attn_o[i], w_ret_o[i], w_out[i],
                         mlp_norm[i], w_up[i], w_down[i], ple_norm[i], w_ple_gate[i], w_ple[i],
                         rope_a, rope_r)
    return rms_norm(x, final_norm)
```

```python
import functools

import jax
import jax.numpy as jnp
from jax import lax
from jax.experimental import pallas as pl
from jax.experimental.pallas import tpu as pltpu

F32 = jnp.float32
BF16 = jnp.bfloat16

GRID_W = 64
ATTN_HEAD_DIM = 64
ATTN_HEADS = 8
ATTN_KV_HEADS = 2
ATTN_GROUP = ATTN_HEADS // ATTN_KV_HEADS
RET_HEAD_DIM = 128
RET_HEADS = 4
ROPE_THETA = 10000.0
NORM_EPS = 1e-6
GN_EPS = 1e-5

LANES = 128
BF16_SUBLANES = 16
V_ROWS = ATTN_HEAD_DIM + BF16_SUBLANES
VMEM_LIMIT = 56 * 1024 * 1024

ROW_TILE = 256
ATTN_Q_TILE = 256
ATTN_K_TILE = 512
RET_CHUNK = 256


def _dot(a, b):
    return jnp.dot(a, b, preferred_element_type=F32)


def _dot_nt(a, b):
    return lax.dot_general(a, b, (((1,), (1,)), ((), ())), preferred_element_type=F32)


def _dot_tn(a, b):
    return lax.dot_general(a, b, (((0,), (0,)), ((), ())), preferred_element_type=F32)


def _rms(x, gain):
    return x * lax.rsqrt(jnp.mean(x * x, axis=-1, keepdims=True) + NORM_EPS) * gain


def _sigmoid(x):
    return 1.0 / (1.0 + jnp.exp(-x))


def _const_spec(shape):
    return pl.BlockSpec(shape, lambda *_: (0,) * len(shape), pipeline_mode=pl.Buffered(1))


def _inproj_kernel(x_ref, g_ref, w_ref, qg_ref, kg_ref, ca_ref, sa_ref, cr_ref, sr_ref, gm_ref,
                   qT_ref, k_ref, vT_ref, qr_ref, kr_ref, vr_ref, gr_ref, sga_ref, sgr_ref):
    tm = x_ref.shape[0]
    h = _rms(x_ref[...], g_ref[...]).astype(BF16)
    lane = lax.broadcasted_iota(jnp.int32, (tm, LANES), 1)
    first_half = (lane % ATTN_HEAD_DIM) < (ATTN_HEAD_DIM // 2)
    ca, sa, cr, sr = ca_ref[...], sa_ref[...], cr_ref[...], sr_ref[...]
    gm = gm_ref[...]

    def proj(c):
        return _dot(h, w_ref[:, c * LANES:(c + 1) * LANES])

    def head_norm_rope(a, gain):
        sq = a * a
        hi = sq.astype(BF16)
        lo = (sq - hi.astype(F32)).astype(BF16)
        ss = _dot(hi, gm) + _dot(lo, gm)
        n = a * lax.rsqrt(ss * (1.0 / ATTN_HEAD_DIM) + NORM_EPS) * gain
        partner = jnp.where(first_half, pltpu.roll(n, LANES - ATTN_HEAD_DIM // 2, 1),
                            pltpu.roll(n, ATTN_HEAD_DIM // 2, 1))
        return n * ca + partner * sa

    def ret_rope(a):
        return a * cr + pltpu.roll(a, RET_HEAD_DIM // 2, 1) * sr

    col = 0
    zeros = jnp.zeros((ATTN_HEAD_DIM, tm), BF16)
    for c in range(ATTN_HEADS * ATTN_HEAD_DIM // LANES):
        q = head_norm_rope(proj(col + c), qg_ref[...]) * (ATTN_HEAD_DIM ** -0.5)
        qt = q.T.astype(BF16)
        for j in range(2):
            head = 2 * c + j
            grp = head // ATTN_GROUP
            rows = qt[j * ATTN_HEAD_DIM:(j + 1) * ATTN_HEAD_DIM]
            qT_ref[head, grp * ATTN_HEAD_DIM:(grp + 1) * ATTN_HEAD_DIM, :] = rows
            qT_ref[head, (1 - grp) * ATTN_HEAD_DIM:(2 - grp) * ATTN_HEAD_DIM, :] = zeros
    col += ATTN_HEADS * ATTN_HEAD_DIM // LANES
    k_ref[...] = head_norm_rope(proj(col), kg_ref[...]).astype(BF16)
    col += 1
    vt = proj(col).T.astype(BF16)
    col += 1
    row = lax.broadcasted_iota(jnp.int32, (BF16_SUBLANES, tm), 0)
    ones_row = jnp.where(row == 0, 1.0, 0.0).astype(BF16)
    for g in range(ATTN_KV_HEADS):
        vT_ref[g, 0, 0:ATTN_HEAD_DIM, :] = vt[g * ATTN_HEAD_DIM:(g + 1) * ATTN_HEAD_DIM]
        vT_ref[g, 0, ATTN_HEAD_DIM:V_ROWS, :] = ones_row
    for c in range(RET_HEADS):
        sl = slice(c * LANES, (c + 1) * LANES)
        qr_ref[:, sl] = (ret_rope(proj(col + c)) * (RET_HEAD_DIM ** -0.5)).astype(BF16)
        kr_ref[:, sl] = ret_rope(proj(col + RET_HEADS + c)).astype(BF16)
        vr_ref[:, sl] = proj(col + 2 * RET_HEADS + c).astype(BF16)
        rg = proj(col + 3 * RET_HEADS + c)
        gr_ref[:, sl] = (rg * _sigmoid(rg)).astype(BF16)
    col += 4 * RET_HEADS
    nd = sga_ref.shape[1] // LANES
    for c in range(nd):
        sl = slice(c * LANES, (c + 1) * LANES)
        sga_ref[:, sl] = _sigmoid(proj(col + c)).astype(BF16)
        sgr_ref[:, sl] = _sigmoid(proj(col + nd + c)).astype(BF16)


def _inproj(x, gain, w, qg, kg, ca, sa, cr, sr, gm, tm):
    s, d = x.shape
    n_in = w.shape[1]
    rw = RET_HEADS * RET_HEAD_DIM
    row = lambda width: pl.BlockSpec((tm, width), lambda i: (i, 0))
    out_shape = (
        jax.ShapeDtypeStruct((ATTN_HEADS, LANES, s), BF16),
        jax.ShapeDtypeStruct((s, LANES), BF16),
        jax.ShapeDtypeStruct((ATTN_KV_HEADS, s // tm, V_ROWS, tm), BF16),
        jax.ShapeDtypeStruct((s, rw), BF16),
        jax.ShapeDtypeStruct((s, rw), BF16),
        jax.ShapeDtypeStruct((s, rw), BF16),
        jax.ShapeDtypeStruct((s, rw), BF16),
        jax.ShapeDtypeStruct((s, d), BF16),
        jax.ShapeDtypeStruct((s, d), BF16),
    )
    out_specs = (
        pl.BlockSpec((ATTN_HEADS, LANES, tm), lambda i: (0, 0, i)),
        row(LANES),
        pl.BlockSpec((ATTN_KV_HEADS, 1, V_ROWS, tm), lambda i: (0, i, 0, 0)),
        row(rw), row(rw), row(rw), row(rw), row(d), row(d),
    )
    in_specs = [
        row(d), _const_spec((1, d)), _const_spec((d, n_in)),
        _const_spec((1, LANES)), _const_spec((1, LANES)),
        row(LANES), row(LANES), row(LANES), row(LANES),
        _const_spec((LANES, LANES)),
    ]
    return pl.pallas_call(
        _inproj_kernel, grid=(s // tm,), in_specs=in_specs, out_specs=out_specs, out_shape=out_shape,
        compiler_params=pltpu.CompilerParams(dimension_semantics=("arbitrary",), vmem_limit_bytes=VMEM_LIMIT),
        name="inproj",
    )(x, gain, w, qg, kg, ca, sa, cr, sr, gm)


def _attn_kernel(qT_ref, k_ref, vT_ref, o_ref, m_scr, acc_scr):
    nk, tk = vT_ref.shape[1], vT_ref.shape[3]
    m_scr[...] = jnp.full(m_scr.shape, -jnp.inf, F32)
    acc_scr[...] = jnp.zeros(acc_scr.shape, F32)

    def body(j, carry):
        kb = k_ref[pl.ds(pl.multiple_of(j * tk, tk), tk), :]
        vb = vT_ref[0, j]
        for h in range(ATTN_GROUP):
            s = _dot(kb, qT_ref[h])
            m_prev = m_scr[h]
            m_new = jnp.maximum(m_prev, jnp.max(s, axis=0, keepdims=True))
            p = jnp.exp(s - m_new).astype(BF16)
            acc_scr[h] = acc_scr[h] * jnp.exp(m_prev - m_new) + _dot(vb, p)
            m_scr[h] = m_new
        return carry

    lax.fori_loop(0, nk, body, 0)
    outs = []
    for h in range(ATTN_GROUP):
        a = acc_scr[h]
        outs.append(a[0:ATTN_HEAD_DIM] * (1.0 / a[ATTN_HEAD_DIM:ATTN_HEAD_DIM + 1]))
    o_ref[...] = jnp.concatenate(outs, axis=0).T.astype(o_ref.dtype)


def _attention(qT, k, vT, tq):
    s = k.shape[0]
    nk, tk = vT.shape[1], vT.shape[3]
    gw = ATTN_GROUP * ATTN_HEAD_DIM
    return pl.pallas_call(
        _attn_kernel,
        grid=(ATTN_KV_HEADS, s // tq),
        in_specs=[
            pl.BlockSpec((ATTN_GROUP, LANES, tq), lambda g, i: (g, 0, i)),
            _const_spec((s, LANES)),
            pl.BlockSpec((1, nk, V_ROWS, tk), lambda g, i: (g, 0, 0, 0)),
        ],
        out_specs=pl.BlockSpec((tq, gw), lambda g, i: (i, g)),
        out_shape=jax.ShapeDtypeStruct((s, ATTN_HEADS * ATTN_HEAD_DIM), BF16),
        scratch_shapes=[pltpu.VMEM((ATTN_GROUP, 1, tq), F32), pltpu.VMEM((ATTN_GROUP, V_ROWS, tq), F32)],
        compiler_params=pltpu.CompilerParams(dimension_semantics=("arbitrary", "arbitrary"),
                                             vmem_limit_bytes=VMEM_LIMIT),
        name="attention",
    )(qT, k, vT)


def _retention_kernel(dl_ref, q_ref, k_ref, v_ref, g_ref, gain_ref, o_ref,
                      sf_scr, sb_scr, sball_scr, dmask_scr, kdf_scr, kdb_scr, qdf_scr, qdb_scr, cd_scr):
    ph, n = pl.program_id(0), pl.program_id(1)
    nc = pl.num_programs(1)
    c = q_ref.shape[0]
    hd = RET_HEAD_DIM

    @pl.when((ph == 0) & (n == 0))
    def _init():
        sf_scr[...] = jnp.zeros(sf_scr.shape, F32)
        sb_scr[...] = jnp.zeros(sb_scr.shape, F32)
        pos = lax.broadcasted_iota(jnp.int32, (c, hd), 0).astype(F32)
        ii = lax.broadcasted_iota(jnp.int32, (c, c), 0)
        jj = lax.broadcasted_iota(jnp.int32, (c, c), 1)
        diff = (ii - jj).astype(F32)

        def log_sigmoid(shape, x):
            v = jnp.full(shape, x, F32)
            return -(jnp.maximum(-v, 0.0) + jnp.log(1.0 + jnp.exp(-jnp.abs(v))))

        for h in range(RET_HEADS):
            lgf, lgb = log_sigmoid((c, hd), dl_ref[0, h]), log_sigmoid((c, hd), dl_ref[1, h])
            kdf_scr[h] = jnp.exp(lgf * (c - 1.0 - pos))
            qdf_scr[h] = jnp.exp(lgf * (pos + 1.0))
            kdb_scr[h] = jnp.exp(lgb * pos)
            qdb_scr[h] = jnp.exp(lgb * (c - pos))
            cd_scr[0, h] = jnp.exp(lgf[0:8] * float(c))
            cd_scr[1, h] = jnp.exp(lgb[0:8] * float(c))
            lgf2, lgb2 = log_sigmoid((c, c), dl_ref[0, h]), log_sigmoid((c, c), dl_ref[1, h])
            dmask_scr[h] = jnp.where(diff >= 0, jnp.exp(lgf2 * jnp.maximum(diff, 0.0)),
                                     jnp.exp(lgb2 * jnp.maximum(-diff, 0.0)))

    @pl.when(ph == 0)
    def _backward_states():
        m = nc - 1 - n
        sball_scr[m] = sb_scr[...].astype(BF16)
        for h in range(RET_HEADS):
            sl = slice(h * hd, (h + 1) * hd)
            kd = (k_ref[:, sl].astype(F32) * kdb_scr[h]).astype(BF16)
            sb_scr[h] = sb_scr[h] * cd_scr[1, h, 0:1, :] + _dot_tn(kd, v_ref[:, sl])

    @pl.when(ph == 1)
    def _forward():
        for h in range(RET_HEADS):
            sl = slice(h * hd, (h + 1) * hd)
            q, k, v = q_ref[:, sl], k_ref[:, sl], v_ref[:, sl]
            qf, kf = q.astype(F32), k.astype(F32)
            a = (_dot_nt(q, k) * dmask_scr[h]).astype(BF16)
            y = _dot(a, v)
            y += _dot((qf * qdf_scr[h]).astype(BF16), sf_scr[h].astype(BF16))
            y += _dot((qf * qdb_scr[h]).astype(BF16), sball_scr[n, h])
            sf_scr[h] = sf_scr[h] * cd_scr[0, h, 0:1, :] + _dot_tn((kf * kdf_scr[h]).astype(BF16), v)
            mu = jnp.mean(y, axis=-1, keepdims=True)
            d = y - mu
            var = jnp.mean(d * d, axis=-1, keepdims=True)
            yn = d * lax.rsqrt(var + GN_EPS)
            o_ref[:, sl] = (yn * gain_ref[:, sl] * g_ref[:, sl].astype(F32)).astype(o_ref.dtype)


def _retention(decay_logit, q, k, v, g, gain, c):
    s, w = q.shape
    nc = s // c
    fwd_only = lambda ph, n: (ph * n, 0)
    both = lambda ph, n: (ph * n + (1 - ph) * (nc - 1 - n), 0)
    hd = RET_HEAD_DIM
    return pl.pallas_call(
        _retention_kernel,
        grid=(2, nc),
        in_specs=[
            pl.BlockSpec(memory_space=pltpu.SMEM),
            pl.BlockSpec((c, w), fwd_only), pl.BlockSpec((c, w), both), pl.BlockSpec((c, w), both),
            pl.BlockSpec((c, w), fwd_only), _const_spec((1, w)),
        ],
        out_specs=pl.BlockSpec((c, w), fwd_only),
        out_shape=jax.ShapeDtypeStruct((s, w), BF16),
        scratch_shapes=[
            pltpu.VMEM((RET_HEADS, hd, hd), F32), pltpu.VMEM((RET_HEADS, hd, hd), F32),
            pltpu.VMEM((nc, RET_HEADS, hd, hd), BF16), pltpu.VMEM((RET_HEADS, c, c), F32),
            pltpu.VMEM((RET_HEADS, c, hd), F32), pltpu.VMEM((RET_HEADS, c, hd), F32),
            pltpu.VMEM((RET_HEADS, c, hd), F32), pltpu.VMEM((RET_HEADS, c, hd), F32),
            pltpu.VMEM((2, RET_HEADS, 8, hd), F32),
        ],
        compiler_params=pltpu.CompilerParams(dimension_semantics=("arbitrary", "arbitrary"),
                                             vmem_limit_bytes=VMEM_LIMIT),
        name="retention",
    )(decay_logit, q, k, v, g, gain)


def _post_kernel(x_ref, o_ref, ry_ref, sga_ref, sgr_ref, p_ref, wao_ref, wro_ref, wout_ref, mg_ref,
                 wup_ref, wdn_ref, pg_ref, wpg_ref, wple_ref, fg_ref, out_ref):
    d = x_ref.shape[1]
    merged = (sga_ref[...].astype(F32) * _dot(o_ref[...], wao_ref[...])
              + sgr_ref[...].astype(F32) * _dot(ry_ref[...], wro_ref[...]))
    x1 = x_ref[...] + _dot(merged.astype(BF16), wout_ref[...])
    hm = _rms(x1, mg_ref[...]).astype(BF16)
    x2 = x1
    for c in range(wup_ref.shape[1] // d):
        sl = slice(c * d, (c + 1) * d)
        u = jnp.maximum(_dot(hm, wup_ref[:, sl]), 0.0)
        x2 = x2 + _dot((u * u).astype(BF16), wdn_ref[sl, :])
    gate = _sigmoid(_dot(_rms(x2, pg_ref[...]).astype(BF16), wpg_ref[...]))
    x3 = x2 + gate * _dot(p_ref[...].astype(BF16), wple_ref[...])
    out_ref[...] = _rms(x3, fg_ref[...])


def _post(x, o, ry, sga, sgr, p, wao, wro, wout, mg, wup, wdn, pg, wpg, wple, fg, tm):
    s, d = x.shape
    row = lambda a: pl.BlockSpec((tm, a.shape[1]), lambda i: (i, 0))
    const = lambda a: _const_spec(a.shape)
    return pl.pallas_call(
        _post_kernel, grid=(s // tm,),
        in_specs=[row(x), row(o), row(ry), row(sga), row(sgr), row(p), const(wao), const(wro), const(wout),
                  const(mg), const(wup), const(wdn), const(pg), const(wpg), const(wple), const(fg)],
        out_specs=pl.BlockSpec((tm, d), lambda i: (i, 0)),
        out_shape=jax.ShapeDtypeStruct((s, d), F32),
        compiler_params=pltpu.CompilerParams(dimension_semantics=("arbitrary",), vmem_limit_bytes=VMEM_LIMIT),
        name="post",
    )(x, o, ry, sga, sgr, p, wao, wro, wout, mg, wup, wdn, pg, wpg, wple, fg)


def _rope_tables(seq_len, head_dim):
    n_axis = head_dim // 4
    freqs = ROPE_THETA ** (-jnp.arange(n_axis, dtype=F32) / n_axis)
    t = jnp.arange(seq_len, dtype=jnp.int32)
    row = (t // GRID_W).astype(F32)
    colp = (t % GRID_W).astype(F32)
    ang = jnp.concatenate([row[:, None] * freqs, colp[:, None] * freqs], axis=-1)
    cos, sin = jnp.cos(ang), jnp.sin(ang)
    reps = LANES // head_dim
    return (jnp.tile(jnp.concatenate([cos, cos], axis=-1), (1, reps)),
            jnp.tile(jnp.concatenate([-sin, sin], axis=-1), (1, reps)))


def _layer(x, p, mix_norm, w_in, attn_q_norm, attn_k_norm, ret_decay_logit, ret_norm_gain, w_attn_o, w_ret_o,
           w_out, mlp_norm, w_up, w_down, ple_norm, w_ple_gate, w_ple, out_gain, tables, tk, tq, c, tm):
    ca, sa, cr, sr = tables
    reps = LANES // ATTN_HEAD_DIM
    lane = jnp.arange(LANES) // ATTN_HEAD_DIM
    gm = (lane[:, None] == lane[None, :]).astype(BF16)
    row2 = lambda a: a.reshape(1, -1)
    qT, k, vT, qr, kr, vr, gr, sga, sgr = _inproj(
        x, row2(mix_norm), w_in.astype(BF16), row2(jnp.tile(attn_q_norm, reps)), row2(jnp.tile(attn_k_norm, reps)),
        ca, sa, cr, sr, gm, tk)
    o = _attention(qT, k, vT, tq)
    ry = _retention(ret_decay_logit, qr, kr, vr, gr, row2(ret_norm_gain), c)
    return _post(x, o, ry, sga, sgr, p, w_attn_o.astype(BF16), w_ret_o.astype(BF16), w_out.astype(BF16),
                 row2(mlp_norm), w_up.astype(BF16), w_down.astype(BF16), row2(ple_norm),
                 w_ple_gate.astype(BF16), w_ple.astype(BF16), row2(out_gain), tm)


def kernel(x, p, mix_norm, w_in, attn_q_norm, attn_k_norm, ret_decay_logit, ret_norm_gain, w_attn_o, w_ret_o,
           w_out, mlp_norm, w_up, w_down, ple_norm, w_ple_gate, w_ple, final_norm):
    b, s, d = x.shape
    depth = p.shape[0]
    assert b == 1 and depth == 1, "single sequence, single layer"
    tables = _rope_tables(s, ATTN_HEAD_DIM) + _rope_tables(s, RET_HEAD_DIM)
    y = _layer(x[0], p[0, 0], mix_norm[0], w_in[0], attn_q_norm[0], attn_k_norm[0], ret_decay_logit[0],
               ret_norm_gain[0], w_attn_o[0], w_ret_o[0], w_out[0], mlp_norm[0], w_up[0], w_down[0], ple_norm[0],
               w_ple_gate[0], w_ple[0], final_norm, tables, min(ATTN_K_TILE, s), min(ATTN_Q_TILE, s),
               min(RET_CHUNK, s), min(ROW_TILE, s))
    return y[None]
```

```python
import functools

import jax
import jax.numpy as jnp
from jax import lax
from jax.experimental import pallas as pl
from jax.experimental.pallas import tpu as pltpu

F32 = jnp.float32
BF16 = jnp.bfloat16

GRID_W = 64
ATTN_HEAD_DIM = 64
ATTN_HEADS = 8
ATTN_KV_HEADS = 2
ATTN_GROUP = ATTN_HEADS // ATTN_KV_HEADS
RET_HEAD_DIM = 128
RET_HEADS = 4
ROPE_THETA = 10000.0
NORM_EPS = 1e-6
GN_EPS = 1e-5
LOG2E = 1.4426950408889634

LANES = 128
BF16_SUBLANES = 16
V_ROWS = ATTN_HEAD_DIM + BF16_SUBLANES
VMEM_LIMIT = 56 * 1024 * 1024

ROW_TILE = 256
ATTN_Q_TILE = 256
ATTN_K_TILE = 512
RET_CHUNK = 256


def _dot(a, b):
    return jnp.dot(a, b, preferred_element_type=F32)


def _dot_nt(a, b):
    return lax.dot_general(a, b, (((1,), (1,)), ((), ())), preferred_element_type=F32)


def _dot_tn(a, b):
    return lax.dot_general(a, b, (((0,), (0,)), ((), ())), preferred_element_type=F32)


def _rms(x, gain):
    return x * lax.rsqrt(jnp.mean(x * x, axis=-1, keepdims=True) + NORM_EPS) * gain


def _sigmoid(x):
    return 1.0 / (1.0 + jnp.exp(-x))


def _const_spec(shape):
    return pl.BlockSpec(shape, lambda *_: (0,) * len(shape), pipeline_mode=pl.Buffered(1))


def _inproj_kernel(x_ref, g_ref, w_ref, qg_ref, kg_ref, ca_ref, sa_ref, cr_ref, sr_ref, gm_ref,
                   qT_ref, k_ref, vT_ref, qr_ref, kr_ref, vr_ref, gr_ref, sga_ref, sgr_ref):
    tm = x_ref.shape[0]
    h = _rms(x_ref[...], g_ref[...]).astype(BF16)
    lane = lax.broadcasted_iota(jnp.int32, (tm, LANES), 1)
    first_half = (lane % ATTN_HEAD_DIM) < (ATTN_HEAD_DIM // 2)
    ca, sa, cr, sr = ca_ref[...], sa_ref[...], cr_ref[...], sr_ref[...]
    gm = gm_ref[...]

    def proj(c):
        return _dot(h, w_ref[:, c * LANES:(c + 1) * LANES])

    def head_norm_rope(a, gain):
        sq = a * a
        hi = sq.astype(BF16)
        lo = (sq - hi.astype(F32)).astype(BF16)
        ss = _dot(hi, gm) + _dot(lo, gm)
        n = a * lax.rsqrt(ss * (1.0 / ATTN_HEAD_DIM) + NORM_EPS) * gain
        partner = jnp.where(first_half, pltpu.roll(n, LANES - ATTN_HEAD_DIM // 2, 1),
                            pltpu.roll(n, ATTN_HEAD_DIM // 2, 1))
        return n * ca + partner * sa

    def ret_rope(a):
        return a * cr + pltpu.roll(a, RET_HEAD_DIM // 2, 1) * sr

    col = 0
    zeros = jnp.zeros((ATTN_HEAD_DIM, tm), BF16)
    for c in range(ATTN_HEADS * ATTN_HEAD_DIM // LANES):
        q = head_norm_rope(proj(col + c), qg_ref[...]) * (ATTN_HEAD_DIM ** -0.5 * LOG2E)
        qt = q.T.astype(BF16)
        for j in range(2):
            head = 2 * c + j
            grp = head // ATTN_GROUP
            rows = qt[j * ATTN_HEAD_DIM:(j + 1) * ATTN_HEAD_DIM]
            qT_ref[head, grp * ATTN_HEAD_DIM:(grp + 1) * ATTN_HEAD_DIM, :] = rows
            qT_ref[head, (1 - grp) * ATTN_HEAD_DIM:(2 - grp) * ATTN_HEAD_DIM, :] = zeros
    col += ATTN_HEADS * ATTN_HEAD_DIM // LANES
    k_ref[...] = head_norm_rope(proj(col), kg_ref[...]).astype(BF16)
    col += 1
    vt = proj(col).T.astype(BF16)
    col += 1
    row = lax.broadcasted_iota(jnp.int32, (BF16_SUBLANES, tm), 0)
    ones_row = jnp.where(row == 0, 1.0, 0.0).astype(BF16)
    for g in range(ATTN_KV_HEADS):
        vT_ref[g, 0, 0:ATTN_HEAD_DIM, :] = vt[g * ATTN_HEAD_DIM:(g + 1) * ATTN_HEAD_DIM]
        vT_ref[g, 0, ATTN_HEAD_DIM:V_ROWS, :] = ones_row
    for c in range(RET_HEADS):
        sl = slice(c * LANES, (c + 1) * LANES)
        qr_ref[:, sl] = (ret_rope(proj(col + c)) * (RET_HEAD_DIM ** -0.5)).astype(BF16)
        kr_ref[:, sl] = ret_rope(proj(col + RET_HEADS + c)).astype(BF16)
        vr_ref[:, sl] = proj(col + 2 * RET_HEADS + c).astype(BF16)
        rg = proj(col + 3 * RET_HEADS + c)
        gr_ref[:, sl] = (rg * _sigmoid(rg)).astype(BF16)
    col += 4 * RET_HEADS
    nd = sga_ref.shape[1] // LANES
    for c in range(nd):
        sl = slice(c * LANES, (c + 1) * LANES)
        sga_ref[:, sl] = _sigmoid(proj(col + c)).astype(BF16)
        sgr_ref[:, sl] = _sigmoid(proj(col + nd + c)).astype(BF16)


def _inproj(x, gain, w, qg, kg, ca, sa, cr, sr, gm, tm):
    s, d = x.shape
    n_in = w.shape[1]
    rw = RET_HEADS * RET_HEAD_DIM
    row = lambda width: pl.BlockSpec((tm, width), lambda i: (i, 0))
    out_shape = (
        jax.ShapeDtypeStruct((ATTN_HEADS, LANES, s), BF16),
        jax.ShapeDtypeStruct((s, LANES), BF16),
        jax.ShapeDtypeStruct((ATTN_KV_HEADS, s // tm, V_ROWS, tm), BF16),
        jax.ShapeDtypeStruct((s, rw), BF16),
        jax.ShapeDtypeStruct((s, rw), BF16),
        jax.ShapeDtypeStruct((s, rw), BF16),
        jax.ShapeDtypeStruct((s, rw), BF16),
        jax.ShapeDtypeStruct((s, d), BF16),
        jax.ShapeDtypeStruct((s, d), BF16),
    )
    out_specs = (
        pl.BlockSpec((ATTN_HEADS, LANES, tm), lambda i: (0, 0, i)),
        row(LANES),
        pl.BlockSpec((ATTN_KV_HEADS, 1, V_ROWS, tm), lambda i: (0, i, 0, 0)),
        row(rw), row(rw), row(rw), row(rw), row(d), row(d),
    )
    in_specs = [
        row(d), _const_spec((1, d)), _const_spec((d, n_in)),
        _const_spec((1, LANES)), _const_spec((1, LANES)),
        row(LANES), row(LANES), row(LANES), row(LANES),
        _const_spec((LANES, LANES)),
    ]
    return pl.pallas_call(
        _inproj_kernel, grid=(s // tm,), in_specs=in_specs, out_specs=out_specs, out_shape=out_shape,
        compiler_params=pltpu.CompilerParams(dimension_semantics=("arbitrary",), vmem_limit_bytes=VMEM_LIMIT),
        name="inproj",
    )(x, gain, w, qg, kg, ca, sa, cr, sr, gm)


def _attn_kernel(qT_ref, k_ref, vT_ref, o_ref, m_scr, acc_scr, s_scr):
    nk, tk = vT_ref.shape[1], vT_ref.shape[3]
    m_scr[...] = jnp.full(m_scr.shape, -jnp.inf, F32)
    acc_scr[...] = jnp.zeros(acc_scr.shape, F32)

    def scores(j, buf):
        kb = k_ref[pl.ds(pl.multiple_of(j * tk, tk), tk), :]
        for h in range(ATTN_GROUP):
            s_scr[buf, h] = _dot(kb, qT_ref[h])

    def update(j, buf):
        vb = vT_ref[0, j]
        ps, alphas = [], []
        for h in range(ATTN_GROUP):
            s = s_scr[buf, h]
            m_prev = m_scr[h]
            m_new = jnp.maximum(m_prev, jnp.max(s, axis=0, keepdims=True))
            ps.append(jnp.exp2(s - m_new).astype(BF16))
            alphas.append(jnp.exp2(m_prev - m_new))
            m_scr[h] = m_new
        for h in range(ATTN_GROUP):
            acc_scr[h] = acc_scr[h] * alphas[h] + _dot(vb, ps[h])

    scores(0, 0)

    def body(i, carry):
        j = 2 * i
        scores(j + 1, 1)
        update(j, 0)
        scores(jnp.minimum(j + 2, nk - 1), 0)
        update(j + 1, 1)
        return carry

    lax.fori_loop(0, nk // 2, body, 0)
    outs = []
    for h in range(ATTN_GROUP):
        a = acc_scr[h]
        outs.append(a[0:ATTN_HEAD_DIM] * (1.0 / a[ATTN_HEAD_DIM:ATTN_HEAD_DIM + 1]))
    o_ref[...] = jnp.concatenate(outs, axis=0).T.astype(o_ref.dtype)


def _attention(qT, k, vT, tq):
    s = k.shape[0]
    nk, tk = vT.shape[1], vT.shape[3]
    gw = ATTN_GROUP * ATTN_HEAD_DIM
    return pl.pallas_call(
        _attn_kernel,
        grid=(ATTN_KV_HEADS, s // tq),
        in_specs=[
            pl.BlockSpec((ATTN_GROUP, LANES, tq), lambda g, i: (g, 0, i)),
            _const_spec((s, LANES)),
            pl.BlockSpec((1, nk, V_ROWS, tk), lambda g, i: (g, 0, 0, 0)),
        ],
        out_specs=pl.BlockSpec((tq, gw), lambda g, i: (i, g)),
        out_shape=jax.ShapeDtypeStruct((s, ATTN_HEADS * ATTN_HEAD_DIM), BF16),
        scratch_shapes=[pltpu.VMEM((ATTN_GROUP, 1, tq), F32), pltpu.VMEM((ATTN_GROUP, V_ROWS, tq), F32),
                        pltpu.VMEM((2, ATTN_GROUP, tk, tq), F32)],
        compiler_params=pltpu.CompilerParams(dimension_semantics=("arbitrary", "arbitrary"),
                                             vmem_limit_bytes=VMEM_LIMIT),
        name="attention",
    )(qT, k, vT)


def _retention_kernel(dl_ref, q_ref, k_ref, v_ref, g_ref, gain_ref, o_ref,
                      sf_scr, sb_scr, sball_scr, dmask_scr, kdf_scr, kdb_scr, qdf_scr, qdb_scr, cd_scr):
    ph, n = pl.program_id(0), pl.program_id(1)
    nc = pl.num_programs(1)
    c = q_ref.shape[0]
    hd = RET_HEAD_DIM

    @pl.when((ph == 0) & (n == 0))
    def _init():
        sf_scr[...] = jnp.zeros(sf_scr.shape, F32)
        sb_scr[...] = jnp.zeros(sb_scr.shape, F32)
        pos = lax.broadcasted_iota(jnp.int32, (c, hd), 0).astype(F32)
        ii = lax.broadcasted_iota(jnp.int32, (c, c), 0)
        jj = lax.broadcasted_iota(jnp.int32, (c, c), 1)
        diff = (ii - jj).astype(F32)

        def log_sigmoid(shape, x):
            v = jnp.full(shape, x, F32)
            return -(jnp.maximum(-v, 0.0) + jnp.log(1.0 + jnp.exp(-jnp.abs(v))))

        for h in range(RET_HEADS):
            lgf, lgb = log_sigmoid((c, hd), dl_ref[0, h]), log_sigmoid((c, hd), dl_ref[1, h])
            kdf_scr[h] = jnp.exp(lgf * (c - 1.0 - pos))
            qdf_scr[h] = jnp.exp(lgf * (pos + 1.0))
            kdb_scr[h] = jnp.exp(lgb * pos)
            qdb_scr[h] = jnp.exp(lgb * (c - pos))
            cd_scr[0, h] = jnp.exp(lgf[0:8] * float(c))
            cd_scr[1, h] = jnp.exp(lgb[0:8] * float(c))
            lgf2, lgb2 = log_sigmoid((c, c), dl_ref[0, h]), log_sigmoid((c, c), dl_ref[1, h])
            dmask_scr[h] = jnp.where(diff >= 0, jnp.exp(lgf2 * jnp.maximum(diff, 0.0)),
                                     jnp.exp(lgb2 * jnp.maximum(-diff, 0.0)))

    @pl.when(ph == 0)
    def _backward_states():
        m = nc - 1 - n
        sball_scr[m] = sb_scr[...].astype(BF16)
        for h in range(RET_HEADS):
            sl = slice(h * hd, (h + 1) * hd)
            kd = (k_ref[:, sl].astype(F32) * kdb_scr[h]).astype(BF16)
            sb_scr[h] = sb_scr[h] * cd_scr[1, h, 0:1, :] + _dot_tn(kd, v_ref[:, sl])

    @pl.when(ph == 1)
    def _forward():
        for h in range(RET_HEADS):
            sl = slice(h * hd, (h + 1) * hd)
            q, k, v = q_ref[:, sl], k_ref[:, sl], v_ref[:, sl]
            qf, kf = q.astype(F32), k.astype(F32)
            a = (_dot_nt(q, k) * dmask_scr[h]).astype(BF16)
            y = _dot(a, v)
            y += _dot((qf * qdf_scr[h]).astype(BF16), sf_scr[h].astype(BF16))
            y += _dot((qf * qdb_scr[h]).astype(BF16), sball_scr[n, h])
            sf_scr[h] = sf_scr[h] * cd_scr[0, h, 0:1, :] + _dot_tn((kf * kdf_scr[h]).astype(BF16), v)
            mu = jnp.mean(y, axis=-1, keepdims=True)
            d = y - mu
            var = jnp.mean(d * d, axis=-1, keepdims=True)
            yn = d * lax.rsqrt(var + GN_EPS)
            o_ref[:, sl] = (yn * gain_ref[:, sl] * g_ref[:, sl].astype(F32)).astype(o_ref.dtype)


def _retention(decay_logit, q, k, v, g, gain, c):
    s, w = q.shape
    nc = s // c
    fwd_only = lambda ph, n: (ph * n, 0)
    both = lambda ph, n: (ph * n + (1 - ph) * (nc - 1 - n), 0)
    hd = RET_HEAD_DIM
    return pl.pallas_call(
        _retention_kernel,
        grid=(2, nc),
        in_specs=[
            pl.BlockSpec(memory_space=pltpu.SMEM),
            pl.BlockSpec((c, w), fwd_only), pl.BlockSpec((c, w), both), pl.BlockSpec((c, w), both),
            pl.BlockSpec((c, w), fwd_only), _const_spec((1, w)),
        ],
        out_specs=pl.BlockSpec((c, w), fwd_only),
        out_shape=jax.ShapeDtypeStruct((s, w), BF16),
        scratch_shapes=[
            pltpu.VMEM((RET_HEADS, hd, hd), F32), pltpu.VMEM((RET_HEADS, hd, hd), F32),
            pltpu.VMEM((nc, RET_HEADS, hd, hd), BF16), pltpu.VMEM((RET_HEADS, c, c), F32),
            pltpu.VMEM((RET_HEADS, c, hd), F32), pltpu.VMEM((RET_HEADS, c, hd), F32),
            pltpu.VMEM((RET_HEADS, c, hd), F32), pltpu.VMEM((RET_HEADS, c, hd), F32),
            pltpu.VMEM((2, RET_HEADS, 8, hd), F32),
        ],
        compiler_params=pltpu.CompilerParams(dimension_semantics=("arbitrary", "arbitrary"),
                                             vmem_limit_bytes=VMEM_LIMIT),
        name="retention",
    )(decay_logit, q, k, v, g, gain)


def _post_kernel(x_ref, o_ref, ry_ref, sga_ref, sgr_ref, p_ref, wao_ref, wro_ref, wout_ref, mg_ref,
                 wup_ref, wdn_ref, pg_ref, wpg_ref, wple_ref, fg_ref, out_ref):
    d = x_ref.shape[1]
    merged = (sga_ref[...].astype(F32) * _dot(o_ref[...], wao_ref[...])
              + sgr_ref[...].astype(F32) * _dot(ry_ref[...], wro_ref[...]))
    x1 = x_ref[...] + _dot(merged.astype(BF16), wout_ref[...])
    hm = _rms(x1, mg_ref[...]).astype(BF16)
    x2 = x1
    for c in range(wup_ref.shape[1] // d):
        sl = slice(c * d, (c + 1) * d)
        u = jnp.maximum(_dot(hm, wup_ref[:, sl]), 0.0)
        x2 = x2 + _dot((u * u).astype(BF16), wdn_ref[sl, :])
    gate = _sigmoid(_dot(_rms(x2, pg_ref[...]).astype(BF16), wpg_ref[...]))
    x3 = x2 + gate * _dot(p_ref[...].astype(BF16), wple_ref[...])
    out_ref[...] = _rms(x3, fg_ref[...])


def _post(x, o, ry, sga, sgr, p, wao, wro, wout, mg, wup, wdn, pg, wpg, wple, fg, tm):
    s, d = x.shape
    row = lambda a: pl.BlockSpec((tm, a.shape[1]), lambda i: (i, 0))
    const = lambda a: _const_spec(a.shape)
    return pl.pallas_call(
        _post_kernel, grid=(s // tm,),
        in_specs=[row(x), row(o), row(ry), row(sga), row(sgr), row(p), const(wao), const(wro), const(wout),
                  const(mg), const(wup), const(wdn), const(pg), const(wpg), const(wple), const(fg)],
        out_specs=pl.BlockSpec((tm, d), lambda i: (i, 0)),
        out_shape=jax.ShapeDtypeStruct((s, d), F32),
        compiler_params=pltpu.CompilerParams(dimension_semantics=("arbitrary",), vmem_limit_bytes=VMEM_LIMIT),
        name="post",
    )(x, o, ry, sga, sgr, p, wao, wro, wout, mg, wup, wdn, pg, wpg, wple, fg)


def _rope_tables(seq_len, head_dim):
    n_axis = head_dim // 4
    freqs = ROPE_THETA ** (-jnp.arange(n_axis, dtype=F32) / n_axis)
    t = jnp.arange(seq_len, dtype=jnp.int32)
    row = (t // GRID_W).astype(F32)
    colp = (t % GRID_W).astype(F32)
    ang = jnp.concatenate([row[:, None] * freqs, colp[:, None] * freqs], axis=-1)
    cos, sin = jnp.cos(ang), jnp.sin(ang)
    reps = LANES // head_dim
    return (jnp.tile(jnp.concatenate([cos, cos], axis=-1), (1, reps)),
            jnp.tile(jnp.concatenate([-sin, sin], axis=-1), (1, reps)))


def _layer(x, p, mix_norm, w_in, attn_q_norm, attn_k_norm, ret_decay_logit, ret_norm_gain, w_attn_o, w_ret_o,
           w_out, mlp_norm, w_up, w_down, ple_norm, w_ple_gate, w_ple, out_gain, tables, tk, tq, c, tm):
    ca, sa, cr, sr = tables
    reps = LANES // ATTN_HEAD_DIM
    lane = jnp.arange(LANES) // ATTN_HEAD_DIM
    gm = (lane[:, None] == lane[None, :]).astype(BF16)
    row2 = lambda a: a.reshape(1, -1)
    qT, k, vT, qr, kr, vr, gr, sga, sgr = _inproj(
        x, row2(mix_norm), w_in.astype(BF16), row2(jnp.tile(attn_q_norm, reps)), row2(jnp.tile(attn_k_norm, reps)),
        ca, sa, cr, sr, gm, tk)
    o = _attention(qT, k, vT, tq)
    ry = _retention(ret_decay_logit, qr, kr, vr, gr, row2(ret_norm_gain), c)
    return _post(x, o, ry, sga, sgr, p, w_attn_o.astype(BF16), w_ret_o.astype(BF16), w_out.astype(BF16),
                 row2(mlp_norm), w_up.astype(BF16), w_down.astype(BF16), row2(ple_norm),
                 w_ple_gate.astype(BF16), w_ple.astype(BF16), row2(out_gain), tm)


def kernel(x, p, mix_norm, w_in, attn_q_norm, attn_k_norm, ret_decay_logit, ret_norm_gain, w_attn_o, w_ret_o,
           w_out, mlp_norm, w_up, w_down, ple_norm, w_ple_gate, w_ple, final_norm):
    b, s, d = x.shape
    depth = p.shape[0]
    assert b == 1 and depth == 1, "single sequence, single layer"
    tables = _rope_tables(s, ATTN_HEAD_DIM) + _rope_tables(s, RET_HEAD_DIM)
    y = _layer(x[0], p[0, 0], mix_norm[0], w_in[0], attn_q_norm[0], attn_k_norm[0], ret_decay_logit[0],
               ret_norm_gain[0], w_attn_o[0], w_ret_o[0], w_out[0], mlp_norm[0], w_up[0], w_down[0], ple_norm[0],
               w_ple_gate[0], w_ple[0], final_norm, tables, min(ATTN_K_TILE, s), min(ATTN_Q_TILE, s),
               min(RET_CHUNK, s), min(ROW_TILE, s))
    return y[None]
```

```python
import functools

import jax
import jax.numpy as jnp
from jax import lax
from jax.experimental import pallas as pl
from jax.experimental.pallas import tpu as pltpu

F32 = jnp.float32
BF16 = jnp.bfloat16

GRID_W = 64
ATTN_HEAD_DIM = 64
ATTN_HEADS = 8
ATTN_KV_HEADS = 2
ATTN_GROUP = ATTN_HEADS // ATTN_KV_HEADS
RET_HEAD_DIM = 128
RET_HEADS = 4
ROPE_THETA = 10000.0
NORM_EPS = 1e-6
GN_EPS = 1e-5
LOG2E = 1.4426950408889634
SHIFT_LIMIT = 100.0

LANES = 128
BF16_SUBLANES = 16
V_ROWS = ATTN_HEAD_DIM + BF16_SUBLANES
VMEM_LIMIT = 56 * 1024 * 1024

ROW_TILE = 256
ATTN_Q_TILE = 256
ATTN_K_TILE = 512
RET_CHUNK = 256


def _dot(a, b):
    return jnp.dot(a, b, preferred_element_type=F32)


def _dot_nt(a, b):
    return lax.dot_general(a, b, (((1,), (1,)), ((), ())), preferred_element_type=F32)


def _dot_tn(a, b):
    return lax.dot_general(a, b, (((0,), (0,)), ((), ())), preferred_element_type=F32)


def _rms(x, gain):
    return x * lax.rsqrt(jnp.mean(x * x, axis=-1, keepdims=True) + NORM_EPS) * gain


def _sigmoid(x):
    return 1.0 / (1.0 + jnp.exp(-x))


def _const_spec(shape):
    return pl.BlockSpec(shape, lambda *_: (0,) * len(shape), pipeline_mode=pl.Buffered(1))


def _inproj_kernel(x_ref, g_ref, w_ref, qg_ref, kg_ref, ca_ref, sa_ref, cr_ref, sr_ref, gm_ref,
                   qT_ref, qn2_ref, k_ref, kn2_ref, vT_ref, qr_ref, kr_ref, vr_ref, gr_ref, sga_ref, sgr_ref):
    tm = x_ref.shape[0]
    h = _rms(x_ref[...], g_ref[...]).astype(BF16)
    lane = lax.broadcasted_iota(jnp.int32, (tm, LANES), 1)
    first_half = (lane % ATTN_HEAD_DIM) < (ATTN_HEAD_DIM // 2)
    ca, sa, cr, sr = ca_ref[...], sa_ref[...], cr_ref[...], sr_ref[...]
    gm = gm_ref[...]

    def proj(c):
        return _dot(h, w_ref[:, c * LANES:(c + 1) * LANES])

    def head_norm_rope(a, gain):
        sq = a * a
        hi = sq.astype(BF16)
        lo = (sq - hi.astype(F32)).astype(BF16)
        ss = _dot(hi, gm) + _dot(lo, gm)
        n = a * lax.rsqrt(ss * (1.0 / ATTN_HEAD_DIM) + NORM_EPS) * gain
        partner = jnp.where(first_half, pltpu.roll(n, LANES - ATTN_HEAD_DIM // 2, 1),
                            pltpu.roll(n, ATTN_HEAD_DIM // 2, 1))
        return n * ca + partner * sa

    def ret_rope(a):
        return a * cr + pltpu.roll(a, RET_HEAD_DIM // 2, 1) * sr

    col = 0
    zeros = jnp.zeros((ATTN_HEAD_DIM, tm), BF16)
    for c in range(ATTN_HEADS * ATTN_HEAD_DIM // LANES):
        q = head_norm_rope(proj(col + c), qg_ref[...]) * (ATTN_HEAD_DIM ** -0.5 * LOG2E)
        qt = q.T.astype(BF16)
        for j in range(2):
            head = 2 * c + j
            rows = qt[j * ATTN_HEAD_DIM:(j + 1) * ATTN_HEAD_DIM]
            qT_ref[head, 0:ATTN_HEAD_DIM, :] = rows
            qT_ref[head, ATTN_HEAD_DIM:LANES, :] = zeros
            rf = rows.astype(F32)
            qn2_ref[head] = jnp.sum(rf * rf, axis=0, keepdims=True)
    col += ATTN_HEADS * ATTN_HEAD_DIM // LANES
    kf = head_norm_rope(proj(col), kg_ref[...]).astype(BF16).astype(F32)
    ksq = kf * kf
    khi = ksq.astype(BF16)
    kss = _dot(khi, gm) + _dot((ksq - khi.astype(F32)).astype(BF16), gm)
    k_sw, kss_sw = pltpu.roll(kf, ATTN_HEAD_DIM, 1), pltpu.roll(kss, ATTN_HEAD_DIM, 1)
    low = lane < ATTN_HEAD_DIM
    one_lane = jnp.where(lane == ATTN_HEAD_DIM, 1.0, 0.0)
    for g in range(ATTN_KV_HEADS):
        k_ref[g] = jnp.where(low, k_sw if g else kf, one_lane).astype(BF16)
        ss_g = jnp.where(low, kss_sw, kss) if g else jnp.where(low, kss, kss_sw)
        kn2_ref[g, 0] = jnp.broadcast_to(jnp.max(ss_g, axis=0, keepdims=True), (8, LANES))
    col += 1
    vt = proj(col).T.astype(BF16)
    col += 1
    row = lax.broadcasted_iota(jnp.int32, (BF16_SUBLANES, tm), 0)
    ones_row = jnp.where(row == 0, 1.0, 0.0).astype(BF16)
    for g in range(ATTN_KV_HEADS):
        vT_ref[g, 0, 0:ATTN_HEAD_DIM, :] = vt[g * ATTN_HEAD_DIM:(g + 1) * ATTN_HEAD_DIM]
        vT_ref[g, 0, ATTN_HEAD_DIM:V_ROWS, :] = ones_row
    for c in range(RET_HEADS):
        sl = slice(c * LANES, (c + 1) * LANES)
        qr_ref[:, sl] = (ret_rope(proj(col + c)) * (RET_HEAD_DIM ** -0.5)).astype(BF16)
        kr_ref[:, sl] = ret_rope(proj(col + RET_HEADS + c)).astype(BF16)
        vr_ref[:, sl] = proj(col + 2 * RET_HEADS + c).astype(BF16)
        rg = proj(col + 3 * RET_HEADS + c)
        gr_ref[:, sl] = (rg * _sigmoid(rg)).astype(BF16)
    col += 4 * RET_HEADS
    nd = sga_ref.shape[1] // LANES
    for c in range(nd):
        sl = slice(c * LANES, (c + 1) * LANES)
        sga_ref[:, sl] = _sigmoid(proj(col + c)).astype(BF16)
        sgr_ref[:, sl] = _sigmoid(proj(col + nd + c)).astype(BF16)


def _inproj(x, gain, w, qg, kg, ca, sa, cr, sr, gm, tm):
    s, d = x.shape
    n_in = w.shape[1]
    rw = RET_HEADS * RET_HEAD_DIM
    row = lambda width: pl.BlockSpec((tm, width), lambda i: (i, 0))
    out_shape = (
        jax.ShapeDtypeStruct((ATTN_HEADS, LANES, s), BF16),
        jax.ShapeDtypeStruct((ATTN_HEADS, 1, s), F32),
        jax.ShapeDtypeStruct((ATTN_KV_HEADS, s, LANES), BF16),
        jax.ShapeDtypeStruct((ATTN_KV_HEADS, s // tm, 8, LANES), F32),
        jax.ShapeDtypeStruct((ATTN_KV_HEADS, s // tm, V_ROWS, tm), BF16),
        jax.ShapeDtypeStruct((s, rw), BF16),
        jax.ShapeDtypeStruct((s, rw), BF16),
        jax.ShapeDtypeStruct((s, rw), BF16),
        jax.ShapeDtypeStruct((s, rw), BF16),
        jax.ShapeDtypeStruct((s, d), BF16),
        jax.ShapeDtypeStruct((s, d), BF16),
    )
    out_specs = (
        pl.BlockSpec((ATTN_HEADS, LANES, tm), lambda i: (0, 0, i)),
        pl.BlockSpec((ATTN_HEADS, 1, tm), lambda i: (0, 0, i)),
        pl.BlockSpec((ATTN_KV_HEADS, tm, LANES), lambda i: (0, i, 0)),
        pl.BlockSpec((ATTN_KV_HEADS, 1, 8, LANES), lambda i: (0, i, 0, 0)),
        pl.BlockSpec((ATTN_KV_HEADS, 1, V_ROWS, tm), lambda i: (0, i, 0, 0)),
        row(rw), row(rw), row(rw), row(rw), row(d), row(d),
    )
    in_specs = [
        row(d), _const_spec((1, d)), _const_spec((d, n_in)),
        _const_spec((1, LANES)), _const_spec((1, LANES)),
        row(LANES), row(LANES), row(LANES), row(LANES),
        _const_spec((LANES, LANES)),
    ]
    return pl.pallas_call(
        _inproj_kernel, grid=(s // tm,), in_specs=in_specs, out_specs=out_specs, out_shape=out_shape,
        compiler_params=pltpu.CompilerParams(dimension_semantics=("arbitrary",), vmem_limit_bytes=VMEM_LIMIT),
        name="inproj",
    )(x, gain, w, qg, kg, ca, sa, cr, sr, gm)


def _attn_kernel(qT_ref, qn2_ref, k_ref, vT_ref, kn2_ref, o_ref, qa_scr, acc_scr, p_scr, m_scr, s_scr):
    nk, tk = vT_ref.shape[1], vT_ref.shape[3]
    tq = qT_ref.shape[2]
    acc_scr[...] = jnp.zeros(acc_scr.shape, F32)

    def key_block(j):
        return k_ref[0, pl.ds(pl.multiple_of(j * tk, tk), tk), :]

    kmax2 = jnp.max(jnp.max(kn2_ref[0], axis=0), axis=0, keepdims=True)
    kmax2 = jnp.concatenate([kmax2] * (tq // LANES), axis=1)
    row = lax.broadcasted_iota(jnp.int32, (BF16_SUBLANES, tq), 0)
    bmax = jnp.zeros((1, tq), F32)
    for h in range(ATTN_GROUP):
        b = jnp.sqrt(qn2_ref[h] * kmax2)
        bmax = jnp.maximum(bmax, b)
        qa_scr[h, 0:ATTN_HEAD_DIM, :] = qT_ref[h, 0:ATTN_HEAD_DIM, :]
        qa_scr[h, ATTN_HEAD_DIM:V_ROWS, :] = jnp.where(row == 0, -b, 0.0).astype(BF16)
        qa_scr[h, V_ROWS:LANES, :] = jnp.zeros((LANES - V_ROWS, tq), BF16)
    shift_ok = 2.0 * jnp.max(bmax) <= SHIFT_LIMIT

    @pl.when(shift_ok)
    def _shifted():
        def step(j, buf):
            kb = key_block(j)
            vb = vT_ref[0, jnp.maximum(j - 1, 0)]
            for h in range(ATTN_GROUP):
                p_scr[buf, h] = jnp.exp2(_dot(kb, qa_scr[h])).astype(BF16)
            for h in range(ATTN_GROUP):
                acc_scr[h] += _dot(vb, p_scr[1 - buf, h])

        p_scr[1] = jnp.zeros(p_scr.shape[1:], BF16)

        def body(i, carry):
            step(2 * i, 0)
            step(2 * i + 1, 1)
            return carry

        lax.fori_loop(0, nk // 2, body, 0)
        vb = vT_ref[0, nk - 1]
        for h in range(ATTN_GROUP):
            acc_scr[h] += _dot(vb, p_scr[1, h])

    @pl.when(jnp.logical_not(shift_ok))
    def _online():
        m_scr[...] = jnp.full(m_scr.shape, -jnp.inf, F32)

        def scores(j, buf):
            kb = key_block(j)
            for h in range(ATTN_GROUP):
                s_scr[buf, h] = _dot(kb, qT_ref[h])

        def update(j, buf):
            vb = vT_ref[0, j]
            ps, alphas = [], []
            for h in range(ATTN_GROUP):
                s = s_scr[buf, h]
                m_prev = m_scr[h]
                m_new = jnp.maximum(m_prev, jnp.max(s, axis=0, keepdims=True))
                ps.append(jnp.exp2(s - m_new).astype(BF16))
                alphas.append(jnp.exp2(m_prev - m_new))
                m_scr[h] = m_new
            for h in range(ATTN_GROUP):
                acc_scr[h] = acc_scr[h] * alphas[h] + _dot(vb, ps[h])

        scores(0, 0)

        def body(i, carry):
            j = 2 * i
            scores(j + 1, 1)
            update(j, 0)
            scores(jnp.minimum(j + 2, nk - 1), 0)
            update(j + 1, 1)
            return carry

        lax.fori_loop(0, nk // 2, body, 0)

    outs = []
    for h in range(ATTN_GROUP):
        a = acc_scr[h]
        outs.append(a[0:ATTN_HEAD_DIM] * (1.0 / a[ATTN_HEAD_DIM:ATTN_HEAD_DIM + 1]))
    o_ref[...] = jnp.concatenate(outs, axis=0).T.astype(o_ref.dtype)


def _attention(qT, qn2, k, vT, kn2, tq):
    s = k.shape[1]
    nk, tk = vT.shape[1], vT.shape[3]
    assert nk % 2 == 0 and tq % LANES == 0
    gw = ATTN_GROUP * ATTN_HEAD_DIM
    return pl.pallas_call(
        _attn_kernel,
        grid=(ATTN_KV_HEADS, s // tq),
        in_specs=[
            pl.BlockSpec((ATTN_GROUP, LANES, tq), lambda g, i: (g, 0, i)),
            pl.BlockSpec((ATTN_GROUP, 1, tq), lambda g, i: (g, 0, i)),
            pl.BlockSpec((1, s, LANES), lambda g, i: (g, 0, 0)),
            pl.BlockSpec((1, nk, V_ROWS, tk), lambda g, i: (g, 0, 0, 0)),
            pl.BlockSpec((1, nk, 8, LANES), lambda g, i: (g, 0, 0, 0)),
        ],
        out_specs=pl.BlockSpec((tq, gw), lambda g, i: (i, g)),
        out_shape=jax.ShapeDtypeStruct((s, ATTN_HEADS * ATTN_HEAD_DIM), BF16),
        scratch_shapes=[pltpu.VMEM((ATTN_GROUP, LANES, tq), BF16), pltpu.VMEM((ATTN_GROUP, V_ROWS, tq), F32),
                        pltpu.VMEM((2, ATTN_GROUP, tk, tq), BF16), pltpu.VMEM((ATTN_GROUP, 1, tq), F32),
                        pltpu.VMEM((2, ATTN_GROUP, tk, tq), F32)],
        compiler_params=pltpu.CompilerParams(dimension_semantics=("arbitrary", "arbitrary"),
                                             vmem_limit_bytes=VMEM_LIMIT),
        name="attention",
    )(qT, qn2, k, vT, kn2)


def _retention_kernel(dl_ref, q_ref, k_ref, v_ref, g_ref, gain_ref, o_ref,
                      sf_scr, sb_scr, sball_scr, dmask_scr, kdf_scr, kdb_scr, qdf_scr, qdb_scr, cd_scr):
    ph, n = pl.program_id(0), pl.program_id(1)
    nc = pl.num_programs(1)
    c = q_ref.shape[0]
    hd = RET_HEAD_DIM

    @pl.when((ph == 0) & (n == 0))
    def _init():
        sf_scr[...] = jnp.zeros(sf_scr.shape, F32)
        sb_scr[...] = jnp.zeros(sb_scr.shape, F32)
        pos = lax.broadcasted_iota(jnp.int32, (c, hd), 0).astype(F32)
        ii = lax.broadcasted_iota(jnp.int32, (c, c), 0)
        jj = lax.broadcasted_iota(jnp.int32, (c, c), 1)
        diff = (ii - jj).astype(F32)

        def log_sigmoid(shape, x):
            v = jnp.full(shape, x, F32)
            return -(jnp.maximum(-v, 0.0) + jnp.log(1.0 + jnp.exp(-jnp.abs(v))))

        for h in range(RET_HEADS):
            lgf, lgb = log_sigmoid((c, hd), dl_ref[0, h]), log_sigmoid((c, hd), dl_ref[1, h])
            kdf_scr[h] = jnp.exp(lgf * (c - 1.0 - pos))
            qdf_scr[h] = jnp.exp(lgf * (pos + 1.0))
            kdb_scr[h] = jnp.exp(lgb * pos)
            qdb_scr[h] = jnp.exp(lgb * (c - pos))
            cd_scr[0, h] = jnp.exp(lgf[0:8] * float(c))
            cd_scr[1, h] = jnp.exp(lgb[0:8] * float(c))
            lgf2, lgb2 = log_sigmoid((c, c), dl_ref[0, h]), log_sigmoid((c, c), dl_ref[1, h])
            dmask_scr[h] = jnp.where(diff >= 0, jnp.exp(lgf2 * jnp.maximum(diff, 0.0)),
                                     jnp.exp(lgb2 * jnp.maximum(-diff, 0.0)))

    @pl.when(ph == 0)
    def _backward_states():
        m = nc - 1 - n
        sball_scr[m] = sb_scr[...].astype(BF16)
        for h in range(RET_HEADS):
            sl = slice(h * hd, (h + 1) * hd)
            kd = (k_ref[:, sl].astype(F32) * kdb_scr[h]).astype(BF16)
            sb_scr[h] = sb_scr[h] * cd_scr[1, h, 0:1, :] + _dot_tn(kd, v_ref[:, sl])

    @pl.when(ph == 1)
    def _forward():
        for h in range(RET_HEADS):
            sl = slice(h * hd, (h + 1) * hd)
            q, k, v = q_ref[:, sl], k_ref[:, sl], v_ref[:, sl]
            qf, kf = q.astype(F32), k.astype(F32)
            a = (_dot_nt(q, k) * dmask_scr[h]).astype(BF16)
            y = _dot(a, v)
            y += _dot((qf * qdf_scr[h]).astype(BF16), sf_scr[h].astype(BF16))
            y += _dot((qf * qdb_scr[h]).astype(BF16), sball_scr[n, h])
            sf_scr[h] = sf_scr[h] * cd_scr[0, h, 0:1, :] + _dot_tn((kf * kdf_scr[h]).astype(BF16), v)
            mu = jnp.mean(y, axis=-1, keepdims=True)
            d = y - mu
            var = jnp.mean(d * d, axis=-1, keepdims=True)
            yn = d * lax.rsqrt(var + GN_EPS)
            o_ref[:, sl] = (yn * gain_ref[:, sl] * g_ref[:, sl].astype(F32)).astype(o_ref.dtype)


def _retention(decay_logit, q, k, v, g, gain, c):
    s, w = q.shape
    nc = s // c
    fwd_only = lambda ph, n: (ph * n, 0)
    both = lambda ph, n: (ph * n + (1 - ph) * (nc - 1 - n), 0)
    hd = RET_HEAD_DIM
    return pl.pallas_call(
        _retention_kernel,
        grid=(2, nc),
        in_specs=[
            pl.BlockSpec(memory_space=pltpu.SMEM),
            pl.BlockSpec((c, w), fwd_only), pl.BlockSpec((c, w), both), pl.BlockSpec((c, w), both),
            pl.BlockSpec((c, w), fwd_only), _const_spec((1, w)),
        ],
        out_specs=pl.BlockSpec((c, w), fwd_only),
        out_shape=jax.ShapeDtypeStruct((s, w), BF16),
        scratch_shapes=[
            pltpu.VMEM((RET_HEADS, hd, hd), F32), pltpu.VMEM((RET_HEADS, hd, hd), F32),
            pltpu.VMEM((nc, RET_HEADS, hd, hd), BF16), pltpu.VMEM((RET_HEADS, c, c), F32),
            pltpu.VMEM((RET_HEADS, c, hd), F32), pltpu.VMEM((RET_HEADS, c, hd), F32),
            pltpu.VMEM((RET_HEADS, c, hd), F32), pltpu.VMEM((RET_HEADS, c, hd), F32),
            pltpu.VMEM((2, RET_HEADS, 8, hd), F32),
        ],
        compiler_params=pltpu.CompilerParams(dimension_semantics=("arbitrary", "arbitrary"),
                                             vmem_limit_bytes=VMEM_LIMIT),
        name="retention",
    )(decay_logit, q, k, v, g, gain)


def _post_kernel(x_ref, o_ref, ry_ref, sga_ref, sgr_ref, p_ref, wao_ref, wro_ref, wout_ref, mg_ref,
                 wup_ref, wdn_ref, pg_ref, wpg_ref, wple_ref, fg_ref, out_ref):
    d = x_ref.shape[1]
    merged = (sga_ref[...].astype(F32) * _dot(o_ref[...], wao_ref[...])
              + sgr_ref[...].astype(F32) * _dot(ry_ref[...], wro_ref[...]))
    x1 = x_ref[...] + _dot(merged.astype(BF16), wout_ref[...])
    hm = _rms(x1, mg_ref[...]).astype(BF16)
    x2 = x1
    for c in range(wup_ref.shape[1] // d):
        sl = slice(c * d, (c + 1) * d)
        u = jnp.maximum(_dot(hm, wup_ref[:, sl]), 0.0)
        x2 = x2 + _dot((u * u).astype(BF16), wdn_ref[sl, :])
    gate = _sigmoid(_dot(_rms(x2, pg_ref[...]).astype(BF16), wpg_ref[...]))
    x3 = x2 + gate * _dot(p_ref[...].astype(BF16), wple_ref[...])
    out_ref[...] = _rms(x3, fg_ref[...])


def _post(x, o, ry, sga, sgr, p, wao, wro, wout, mg, wup, wdn, pg, wpg, wple, fg, tm):
    s, d = x.shape
    row = lambda a: pl.BlockSpec((tm, a.shape[1]), lambda i: (i, 0))
    const = lambda a: _const_spec(a.shape)
    return pl.pallas_call(
        _post_kernel, grid=(s // tm,),
        in_specs=[row(x), row(o), row(ry), row(sga), row(sgr), row(p), const(wao), const(wro), const(wout),
                  const(mg), const(wup), const(wdn), const(pg), const(wpg), const(wple), const(fg)],
        out_specs=pl.BlockSpec((tm, d), lambda i: (i, 0)),
        out_shape=jax.ShapeDtypeStruct((s, d), F32),
        compiler_params=pltpu.CompilerParams(dimension_semantics=("arbitrary",), vmem_limit_bytes=VMEM_LIMIT),
        name="post",
    )(x, o, ry, sga, sgr, p, wao, wro, wout, mg, wup, wdn, pg, wpg, wple, fg)


def _rope_tables(seq_len, head_dim):
    n_axis = head_dim // 4
    freqs = ROPE_THETA ** (-jnp.arange(n_axis, dtype=F32) / n_axis)
    t = jnp.arange(seq_len, dtype=jnp.int32)
    row = (t // GRID_W).astype(F32)
    colp = (t % GRID_W).astype(F32)
    ang = jnp.concatenate([row[:, None] * freqs, colp[:, None] * freqs], axis=-1)
    cos, sin = jnp.cos(ang), jnp.sin(ang)
    reps = LANES // head_dim
    return (jnp.tile(jnp.concatenate([cos, cos], axis=-1), (1, reps)),
            jnp.tile(jnp.concatenate([-sin, sin], axis=-1), (1, reps)))


def _layer(x, p, mix_norm, w_in, attn_q_norm, attn_k_norm, ret_decay_logit, ret_norm_gain, w_attn_o, w_ret_o,
           w_out, mlp_norm, w_up, w_down, ple_norm, w_ple_gate, w_ple, out_gain, tables, tk, tq, c, tm):
    ca, sa, cr, sr = tables
    reps = LANES // ATTN_HEAD_DIM
    lane = jnp.arange(LANES) // ATTN_HEAD_DIM
    gm = (lane[:, None] == lane[None, :]).astype(BF16)
    row2 = lambda a: a.reshape(1, -1)
    qT, qn2, k, kn2, vT, qr, kr, vr, gr, sga, sgr = _inproj(
        x, row2(mix_norm), w_in.astype(BF16), row2(jnp.tile(attn_q_norm, reps)), row2(jnp.tile(attn_k_norm, reps)),
        ca, sa, cr, sr, gm, tk)
    o = _attention(qT, qn2, k, vT, kn2, tq)
    ry = _retention(ret_decay_logit, qr, kr, vr, gr, row2(ret_norm_gain), c)
    return _post(x, o, ry, sga, sgr, p, w_attn_o.astype(BF16), w_ret_o.astype(BF16), w_out.astype(BF16),
                 row2(mlp_norm), w_up.astype(BF16), w_down.astype(BF16), row2(ple_norm),
                 w_ple_gate.astype(BF16), w_ple.astype(BF16), row2(out_gain), tm)


def kernel(x, p, mix_norm, w_in, attn_q_norm, attn_k_norm, ret_decay_logit, ret_norm_gain, w_attn_o, w_ret_o,
           w_out, mlp_norm, w_up, w_down, ple_norm, w_ple_gate, w_ple, final_norm):
    b, s, d = x.shape
    depth = p.shape[0]
    assert b == 1 and depth == 1, "single sequence, single layer"
    tables = _rope_tables(s, ATTN_HEAD_DIM) + _rope_tables(s, RET_HEAD_DIM)
    y = _layer(x[0], p[0, 0], mix_norm[0], w_in[0], attn_q_norm[0], attn_k_norm[0], ret_decay_logit[0],
               ret_norm_gain[0], w_attn_o[0], w_ret_o[0], w_out[0], mlp_norm[0], w_up[0], w_down[0], ple_norm[0],
               w_ple_gate[0], w_ple[0], final_norm, tables, min(ATTN_K_TILE, s), min(ATTN_Q_TILE, s),
               min(RET_CHUNK, s), min(ROW_TILE, s))
    return y[None]
```

```python
import functools

import jax
import jax.numpy as jnp
from jax import lax
from jax.experimental import pallas as pl
from jax.experimental.pallas import tpu as pltpu

F32 = jnp.float32
BF16 = jnp.bfloat16

GRID_W = 64
ATTN_HEAD_DIM = 64
ATTN_HEADS = 8
ATTN_KV_HEADS = 2
ATTN_GROUP = ATTN_HEADS // ATTN_KV_HEADS
RET_HEAD_DIM = 128
RET_HEADS = 4
ROPE_THETA = 10000.0
NORM_EPS = 1e-6
GN_EPS = 1e-5
LOG2E = 1.4426950408889634
SHIFT_LIMIT = 100.0

LANES = 128
MXU_COLS = 256
BF16_SUBLANES = 16
V_ROWS = ATTN_HEAD_DIM + BF16_SUBLANES
VMEM_LIMIT = 56 * 1024 * 1024

ROW_TILE = 256
ATTN_Q_TILE = 256
ATTN_K_TILE = 512
RET_CHUNK = 256


def _dot(a, b):
    return jnp.dot(a, b, preferred_element_type=F32)


def _dot_nt(a, b):
    return lax.dot_general(a, b, (((1,), (1,)), ((), ())), preferred_element_type=F32)


def _dot_tn(a, b):
    return lax.dot_general(a, b, (((0,), (0,)), ((), ())), preferred_element_type=F32)


def _rms(x, gain):
    return x * lax.rsqrt(jnp.mean(x * x, axis=-1, keepdims=True) + NORM_EPS) * gain


def _sigmoid(x):
    return 1.0 / (1.0 + jnp.exp(-x))


def _const_spec(shape):
    return pl.BlockSpec(shape, lambda *_: (0,) * len(shape), pipeline_mode=pl.Buffered(1))


def _inproj_kernel(x_ref, g_ref, w_ref, qg_ref, kg_ref, ca_ref, sa_ref, cr_ref, sr_ref, gm_ref,
                   qT_ref, qn2_ref, k_ref, kn2_ref, vT_ref, qr_ref, kr_ref, vr_ref, gr_ref, sga_ref, sgr_ref):
    tm = x_ref.shape[0]
    h = _rms(x_ref[...], g_ref[...]).astype(BF16)
    lane = lax.broadcasted_iota(jnp.int32, (tm, LANES), 1)
    first_half = (lane % ATTN_HEAD_DIM) < (ATTN_HEAD_DIM // 2)
    ca, sa, cr, sr = ca_ref[...], sa_ref[...], cr_ref[...], sr_ref[...]
    gm = gm_ref[...]

    def proj_pair(c):
        p = _dot(h, w_ref[:, c * LANES:c * LANES + MXU_COLS])
        return p[:, :LANES], p[:, LANES:]

    def head_norm_rope(a, gain):
        sq = a * a
        hi = sq.astype(BF16)
        lo = (sq - hi.astype(F32)).astype(BF16)
        ss = _dot(hi, gm) + _dot(lo, gm)
        n = a * lax.rsqrt(ss * (1.0 / ATTN_HEAD_DIM) + NORM_EPS) * gain
        partner = jnp.where(first_half, pltpu.roll(n, LANES - ATTN_HEAD_DIM // 2, 1),
                            pltpu.roll(n, ATTN_HEAD_DIM // 2, 1))
        return n * ca + partner * sa

    def ret_rope(a):
        return a * cr + pltpu.roll(a, RET_HEAD_DIM // 2, 1) * sr

    col = 0
    zeros = jnp.zeros((ATTN_HEAD_DIM, tm), BF16)
    for c0 in range(0, ATTN_HEADS * ATTN_HEAD_DIM // LANES, 2):
        for c, a in zip((c0, c0 + 1), proj_pair(col + c0)):
            q = head_norm_rope(a, qg_ref[...]) * (ATTN_HEAD_DIM ** -0.5 * LOG2E)
            qt = q.T.astype(BF16)
            for j in range(2):
                head = 2 * c + j
                rows = qt[j * ATTN_HEAD_DIM:(j + 1) * ATTN_HEAD_DIM]
                qT_ref[head, 0:ATTN_HEAD_DIM, :] = rows
                qT_ref[head, ATTN_HEAD_DIM:LANES, :] = zeros
                rf = rows.astype(F32)
                qn2_ref[head] = jnp.sum(rf * rf, axis=0, keepdims=True)
    col += ATTN_HEADS * ATTN_HEAD_DIM // LANES
    k_slab, v_slab = proj_pair(col)
    col += 2
    kf = head_norm_rope(k_slab, kg_ref[...]).astype(BF16).astype(F32)
    ksq = kf * kf
    khi = ksq.astype(BF16)
    kss = _dot(khi, gm) + _dot((ksq - khi.astype(F32)).astype(BF16), gm)
    k_sw, kss_sw = pltpu.roll(kf, ATTN_HEAD_DIM, 1), pltpu.roll(kss, ATTN_HEAD_DIM, 1)
    low = lane < ATTN_HEAD_DIM
    one_lane = jnp.where(lane == ATTN_HEAD_DIM, 1.0, 0.0)
    for g in range(ATTN_KV_HEADS):
        k_ref[g] = jnp.where(low, k_sw if g else kf, one_lane).astype(BF16)
        ss_g = jnp.where(low, kss_sw, kss) if g else jnp.where(low, kss, kss_sw)
        kn2_ref[g, 0] = jnp.broadcast_to(jnp.max(ss_g, axis=0, keepdims=True), (8, LANES))
    vt = v_slab.T.astype(BF16)
    row = lax.broadcasted_iota(jnp.int32, (BF16_SUBLANES, tm), 0)
    ones_row = jnp.where(row == 0, 1.0, 0.0).astype(BF16)
    for g in range(ATTN_KV_HEADS):
        vT_ref[g, 0, 0:ATTN_HEAD_DIM, :] = vt[g * ATTN_HEAD_DIM:(g + 1) * ATTN_HEAD_DIM]
        vT_ref[g, 0, ATTN_HEAD_DIM:V_ROWS, :] = ones_row
    def slabs(first, count, out_ref, fn):
        for c0 in range(0, count, 2):
            for c, a in zip((c0, c0 + 1), proj_pair(first + c0)):
                out_ref[:, c * LANES:(c + 1) * LANES] = fn(a).astype(BF16)

    slabs(col, RET_HEADS, qr_ref, lambda a: ret_rope(a) * (RET_HEAD_DIM ** -0.5))
    slabs(col + RET_HEADS, RET_HEADS, kr_ref, ret_rope)
    slabs(col + 2 * RET_HEADS, RET_HEADS, vr_ref, lambda a: a)
    slabs(col + 3 * RET_HEADS, RET_HEADS, gr_ref, lambda a: a * _sigmoid(a))
    col += 4 * RET_HEADS
    nd = sga_ref.shape[1] // LANES
    slabs(col, nd, sga_ref, _sigmoid)
    slabs(col + nd, nd, sgr_ref, _sigmoid)


def _inproj(x, gain, w, qg, kg, ca, sa, cr, sr, gm, tm):
    s, d = x.shape
    n_in = w.shape[1]
    rw = RET_HEADS * RET_HEAD_DIM
    row = lambda width: pl.BlockSpec((tm, width), lambda i: (i, 0))
    out_shape = (
        jax.ShapeDtypeStruct((ATTN_HEADS, LANES, s), BF16),
        jax.ShapeDtypeStruct((ATTN_HEADS, 1, s), F32),
        jax.ShapeDtypeStruct((ATTN_KV_HEADS, s, LANES), BF16),
        jax.ShapeDtypeStruct((ATTN_KV_HEADS, s // tm, 8, LANES), F32),
        jax.ShapeDtypeStruct((ATTN_KV_HEADS, s // tm, V_ROWS, tm), BF16),
        jax.ShapeDtypeStruct((s, rw), BF16),
        jax.ShapeDtypeStruct((s, rw), BF16),
        jax.ShapeDtypeStruct((s, rw), BF16),
        jax.ShapeDtypeStruct((s, rw), BF16),
        jax.ShapeDtypeStruct((s, d), BF16),
        jax.ShapeDtypeStruct((s, d), BF16),
    )
    out_specs = (
        pl.BlockSpec((ATTN_HEADS, LANES, tm), lambda i: (0, 0, i)),
        pl.BlockSpec((ATTN_HEADS, 1, tm), lambda i: (0, 0, i)),
        pl.BlockSpec((ATTN_KV_HEADS, tm, LANES), lambda i: (0, i, 0)),
        pl.BlockSpec((ATTN_KV_HEADS, 1, 8, LANES), lambda i: (0, i, 0, 0)),
        pl.BlockSpec((ATTN_KV_HEADS, 1, V_ROWS, tm), lambda i: (0, i, 0, 0)),
        row(rw), row(rw), row(rw), row(rw), row(d), row(d),
    )
    in_specs = [
        row(d), _const_spec((1, d)), _const_spec((d, n_in)),
        _const_spec((1, LANES)), _const_spec((1, LANES)),
        row(LANES), row(LANES), row(LANES), row(LANES),
        _const_spec((LANES, LANES)),
    ]
    return pl.pallas_call(
        _inproj_kernel, grid=(s // tm,), in_specs=in_specs, out_specs=out_specs, out_shape=out_shape,
        compiler_params=pltpu.CompilerParams(dimension_semantics=("arbitrary",), vmem_limit_bytes=VMEM_LIMIT),
        name="inproj",
    )(x, gain, w, qg, kg, ca, sa, cr, sr, gm)


def _attn_kernel(qT_ref, qn2_ref, k_ref, vT_ref, kn2_ref, o_ref, qa_scr, acc_scr, p_scr, m_scr, s_scr):
    nk, tk = vT_ref.shape[1], vT_ref.shape[3]
    tq = qT_ref.shape[2]
    acc_scr[...] = jnp.zeros(acc_scr.shape, F32)

    def key_block(j):
        return k_ref[0, pl.ds(pl.multiple_of(j * tk, tk), tk), :]

    kmax2 = jnp.max(jnp.max(kn2_ref[0], axis=0), axis=0, keepdims=True)
    kmax2 = jnp.concatenate([kmax2] * (tq // LANES), axis=1)
    row = lax.broadcasted_iota(jnp.int32, (BF16_SUBLANES, tq), 0)
    bmax = jnp.zeros((1, tq), F32)
    for h in range(ATTN_GROUP):
        b = jnp.sqrt(qn2_ref[h] * kmax2)
        bmax = jnp.maximum(bmax, b)
        qa_scr[h, 0:ATTN_HEAD_DIM, :] = qT_ref[h, 0:ATTN_HEAD_DIM, :]
        qa_scr[h, ATTN_HEAD_DIM:V_ROWS, :] = jnp.where(row == 0, -b, 0.0).astype(BF16)
        qa_scr[h, V_ROWS:LANES, :] = jnp.zeros((LANES - V_ROWS, tq), BF16)
    shift_ok = 2.0 * jnp.max(bmax) <= SHIFT_LIMIT

    @pl.when(shift_ok)
    def _shifted():
        def step(j, buf):
            kb = key_block(j)
            vb = vT_ref[0, jnp.maximum(j - 1, 0)]
            for h in range(ATTN_GROUP):
                p_scr[buf, h] = jnp.exp2(_dot(kb, qa_scr[h])).astype(BF16)
            for h in range(ATTN_GROUP):
                acc_scr[h] += _dot(vb, p_scr[1 - buf, h])

        p_scr[1] = jnp.zeros(p_scr.shape[1:], BF16)

        unroll = 4 if nk % 4 == 0 else 2

        def body(i, carry):
            for u in range(unroll):
                step(unroll * i + u, u % 2)
            return carry

        lax.fori_loop(0, nk // unroll, body, 0)
        vb = vT_ref[0, nk - 1]
        for h in range(ATTN_GROUP):
            acc_scr[h] += _dot(vb, p_scr[1, h])

    @pl.when(jnp.logical_not(shift_ok))
    def _online():
        m_scr[...] = jnp.full(m_scr.shape, -jnp.inf, F32)

        def scores(j, buf):
            kb = key_block(j)
            for h in range(ATTN_GROUP):
                s_scr[buf, h] = _dot(kb, qT_ref[h])

        def update(j, buf):
            vb = vT_ref[0, j]
            ps, alphas = [], []
            for h in range(ATTN_GROUP):
                s = s_scr[buf, h]
                m_prev = m_scr[h]
                m_new = jnp.maximum(m_prev, jnp.max(s, axis=0, keepdims=True))
                ps.append(jnp.exp2(s - m_new).astype(BF16))
                alphas.append(jnp.exp2(m_prev - m_new))
                m_scr[h] = m_new
            for h in range(ATTN_GROUP):
                acc_scr[h] = acc_scr[h] * alphas[h] + _dot(vb, ps[h])

        scores(0, 0)

        def body(i, carry):
            j = 2 * i
            scores(j + 1, 1)
            update(j, 0)
            scores(jnp.minimum(j + 2, nk - 1), 0)
            update(j + 1, 1)
            return carry

        lax.fori_loop(0, nk // 2, body, 0)

    outs = []
    for h in range(ATTN_GROUP):
        a = acc_scr[h]
        outs.append(a[0:ATTN_HEAD_DIM] * (1.0 / a[ATTN_HEAD_DIM:ATTN_HEAD_DIM + 1]))
    o_ref[...] = jnp.concatenate(outs, axis=0).T.astype(o_ref.dtype)


def _attention(qT, qn2, k, vT, kn2, tq):
    s = k.shape[1]
    nk, tk = vT.shape[1], vT.shape[3]
    assert nk % 2 == 0 and tq % LANES == 0
    gw = ATTN_GROUP * ATTN_HEAD_DIM
    return pl.pallas_call(
        _attn_kernel,
        grid=(ATTN_KV_HEADS, s // tq),
        in_specs=[
            pl.BlockSpec((ATTN_GROUP, LANES, tq), lambda g, i: (g, 0, i)),
            pl.BlockSpec((ATTN_GROUP, 1, tq), lambda g, i: (g, 0, i)),
            pl.BlockSpec((1, s, LANES), lambda g, i: (g, 0, 0)),
            pl.BlockSpec((1, nk, V_ROWS, tk), lambda g, i: (g, 0, 0, 0)),
            pl.BlockSpec((1, nk, 8, LANES), lambda g, i: (g, 0, 0, 0)),
        ],
        out_specs=pl.BlockSpec((tq, gw), lambda g, i: (i, g)),
        out_shape=jax.ShapeDtypeStruct((s, ATTN_HEADS * ATTN_HEAD_DIM), BF16),
        scratch_shapes=[pltpu.VMEM((ATTN_GROUP, LANES, tq), BF16), pltpu.VMEM((ATTN_GROUP, V_ROWS, tq), F32),
                        pltpu.VMEM((2, ATTN_GROUP, tk, tq), BF16), pltpu.VMEM((ATTN_GROUP, 1, tq), F32),
                        pltpu.VMEM((2, ATTN_GROUP, tk, tq), F32)],
        compiler_params=pltpu.CompilerParams(dimension_semantics=("arbitrary", "arbitrary"),
                                             vmem_limit_bytes=VMEM_LIMIT),
        name="attention",
    )(qT, qn2, k, vT, kn2)


def _retention_kernel(dl_ref, q_ref, k_ref, v_ref, g_ref, gain_ref, o_ref,
                      sf_scr, sb_scr, sball_scr, dmask_scr, kdf_scr, kdb_scr, qdf_scr, qdb_scr, cd_scr):
    ph, n = pl.program_id(0), pl.program_id(1)
    nc = pl.num_programs(1)
    c = q_ref.shape[0]
    hd = RET_HEAD_DIM

    @pl.when((ph == 0) & (n == 0))
    def _init():
        sf_scr[...] = jnp.zeros(sf_scr.shape, F32)
        sb_scr[...] = jnp.zeros(sb_scr.shape, F32)
        pos = lax.broadcasted_iota(jnp.int32, (c, hd), 0).astype(F32)
        ii = lax.broadcasted_iota(jnp.int32, (c, c), 0)
        jj = lax.broadcasted_iota(jnp.int32, (c, c), 1)
        diff = (ii - jj).astype(F32)

        def log_sigmoid(shape, x):
            v = jnp.full(shape, x, F32)
            return -(jnp.maximum(-v, 0.0) + jnp.log(1.0 + jnp.exp(-jnp.abs(v))))

        for h in range(RET_HEADS):
            lgf, lgb = log_sigmoid((c, hd), dl_ref[0, h]), log_sigmoid((c, hd), dl_ref[1, h])
            kdf_scr[h] = jnp.exp(lgf * (c - 1.0 - pos))
            qdf_scr[h] = jnp.exp(lgf * (pos + 1.0))
            kdb_scr[h] = jnp.exp(lgb * pos)
            qdb_scr[h] = jnp.exp(lgb * (c - pos))
            cd_scr[0, h] = jnp.exp(lgf[0:8] * float(c))
            cd_scr[1, h] = jnp.exp(lgb[0:8] * float(c))
            lgf2, lgb2 = log_sigmoid((c, c), dl_ref[0, h]), log_sigmoid((c, c), dl_ref[1, h])
            dmask_scr[h] = jnp.where(diff >= 0, jnp.exp(lgf2 * jnp.maximum(diff, 0.0)),
                                     jnp.exp(lgb2 * jnp.maximum(-diff, 0.0)))

    @pl.when(ph == 0)
    def _backward_states():
        m = nc - 1 - n
        sball_scr[m] = sb_scr[...].astype(BF16)
        for h in range(RET_HEADS):
            sl = slice(h * hd, (h + 1) * hd)
            kd = (k_ref[:, sl].astype(F32) * kdb_scr[h]).astype(BF16)
            sb_scr[h] = sb_scr[h] * cd_scr[1, h, 0:1, :] + _dot_tn(kd, v_ref[:, sl])

    @pl.when(ph == 1)
    def _forward():
        for h in range(RET_HEADS):
            sl = slice(h * hd, (h + 1) * hd)
            q, k, v = q_ref[:, sl], k_ref[:, sl], v_ref[:, sl]
            qf, kf = q.astype(F32), k.astype(F32)
            a = (_dot_nt(q, k) * dmask_scr[h]).astype(BF16)
            y = _dot(a, v)
            y += _dot((qf * qdf_scr[h]).astype(BF16), sf_scr[h].astype(BF16))
            y += _dot((qf * qdb_scr[h]).astype(BF16), sball_scr[n, h])
            sf_scr[h] = sf_scr[h] * cd_scr[0, h, 0:1, :] + _dot_tn((kf * kdf_scr[h]).astype(BF16), v)
            mu = jnp.mean(y, axis=-1, keepdims=True)
            d = y - mu
            var = jnp.mean(d * d, axis=-1, keepdims=True)
            yn = d * lax.rsqrt(var + GN_EPS)
            o_ref[:, sl] = (yn * gain_ref[:, sl] * g_ref[:, sl].astype(F32)).astype(o_ref.dtype)


def _retention(decay_logit, q, k, v, g, gain, c):
    s, w = q.shape
    nc = s // c
    fwd_only = lambda ph, n: (ph * n, 0)
    both = lambda ph, n: (ph * n + (1 - ph) * (nc - 1 - n), 0)
    hd = RET_HEAD_DIM
    return pl.pallas_call(
        _retention_kernel,
        grid=(2, nc),
        in_specs=[
            pl.BlockSpec(memory_space=pltpu.SMEM),
            pl.BlockSpec((c, w), fwd_only), pl.BlockSpec((c, w), both), pl.BlockSpec((c, w), both),
            pl.BlockSpec((c, w), fwd_only), _const_spec((1, w)),
        ],
        out_specs=pl.BlockSpec((c, w), fwd_only),
        out_shape=jax.ShapeDtypeStruct((s, w), BF16),
        scratch_shapes=[
            pltpu.VMEM((RET_HEADS, hd, hd), F32), pltpu.VMEM((RET_HEADS, hd, hd), F32),
            pltpu.VMEM((nc, RET_HEADS, hd, hd), BF16), pltpu.VMEM((RET_HEADS, c, c), F32),
            pltpu.VMEM((RET_HEADS, c, hd), F32), pltpu.VMEM((RET_HEADS, c, hd), F32),
            pltpu.VMEM((RET_HEADS, c, hd), F32), pltpu.VMEM((RET_HEADS, c, hd), F32),
            pltpu.VMEM((2, RET_HEADS, 8, hd), F32),
        ],
        compiler_params=pltpu.CompilerParams(dimension_semantics=("arbitrary", "arbitrary"),
                                             vmem_limit_bytes=VMEM_LIMIT),
        name="retention",
    )(decay_logit, q, k, v, g, gain)


def _post_kernel(x_ref, o_ref, ry_ref, sga_ref, sgr_ref, p_ref, wao_ref, wro_ref, wout_ref, mg_ref,
                 wup_ref, wdn_ref, pg_ref, wpg_ref, wple_ref, fg_ref, out_ref):
    d = x_ref.shape[1]
    merged = (sga_ref[...].astype(F32) * _dot(o_ref[...], wao_ref[...])
              + sgr_ref[...].astype(F32) * _dot(ry_ref[...], wro_ref[...]))
    x1 = x_ref[...] + _dot(merged.astype(BF16), wout_ref[...])
    hm = _rms(x1, mg_ref[...]).astype(BF16)
    x2 = x1
    for c in range(wup_ref.shape[1] // d):
        sl = slice(c * d, (c + 1) * d)
        u = jnp.maximum(_dot(hm, wup_ref[:, sl]), 0.0)
        x2 = x2 + _dot((u * u).astype(BF16), wdn_ref[sl, :])
    gate = _sigmoid(_dot(_rms(x2, pg_ref[...]).astype(BF16), wpg_ref[...]))
    x3 = x2 + gate * _dot(p_ref[...].astype(BF16), wple_ref[...])
    out_ref[...] = _rms(x3, fg_ref[...])


def _post(x, o, ry, sga, sgr, p, wao, wro, wout, mg, wup, wdn, pg, wpg, wple, fg, tm):
    s, d = x.shape
    row = lambda a: pl.BlockSpec((tm, a.shape[1]), lambda i: (i, 0))
    const = lambda a: _const_spec(a.shape)
    return pl.pallas_call(
        _post_kernel, grid=(s // tm,),
        in_specs=[row(x), row(o), row(ry), row(sga), row(sgr), row(p), const(wao), const(wro), const(wout),
                  const(mg), const(wup), const(wdn), const(pg), const(wpg), const(wple), const(fg)],
        out_specs=pl.BlockSpec((tm, d), lambda i: (i, 0)),
        out_shape=jax.ShapeDtypeStruct((s, d), F32),
        compiler_params=pltpu.CompilerParams(dimension_semantics=("arbitrary",), vmem_limit_bytes=VMEM_LIMIT),
        name="post",
    )(x, o, ry, sga, sgr, p, wao, wro, wout, mg, wup, wdn, pg, wpg, wple, fg)


def _rope_tables(seq_len, head_dim):
    n_axis = head_dim // 4
    freqs = ROPE_THETA ** (-jnp.arange(n_axis, dtype=F32) / n_axis)
    t = jnp.arange(seq_len, dtype=jnp.int32)
    row = (t // GRID_W).astype(F32)
    colp = (t % GRID_W).astype(F32)
    ang = jnp.concatenate([row[:, None] * freqs, colp[:, None] * freqs], axis=-1)
    cos, sin = jnp.cos(ang), jnp.sin(ang)
    reps = LANES // head_dim
    return (jnp.tile(jnp.concatenate([cos, cos], axis=-1), (1, reps)),
            jnp.tile(jnp.concatenate([-sin, sin], axis=-1), (1, reps)))


def _layer(x, p, mix_norm, w_in, attn_q_norm, attn_k_norm, ret_decay_logit, ret_norm_gain, w_attn_o, w_ret_o,
           w_out, mlp_norm, w_up, w_down, ple_norm, w_ple_gate, w_ple, out_gain, tables, tk, tq, c, tm):
    ca, sa, cr, sr = tables
    reps = LANES // ATTN_HEAD_DIM
    lane = jnp.arange(LANES) // ATTN_HEAD_DIM
    gm = (lane[:, None] == lane[None, :]).astype(BF16)
    row2 = lambda a: a.reshape(1, -1)
    qT, qn2, k, kn2, vT, qr, kr, vr, gr, sga, sgr = _inproj(
        x, row2(mix_norm), w_in.astype(BF16), row2(jnp.tile(attn_q_norm, reps)), row2(jnp.tile(attn_k_norm, reps)),
        ca, sa, cr, sr, gm, tk)
    o = _attention(qT, qn2, k, vT, kn2, tq)
    ry = _retention(ret_decay_logit, qr, kr, vr, gr, row2(ret_norm_gain), c)
    return _post(x, o, ry, sga, sgr, p, w_attn_o.astype(BF16), w_ret_o.astype(BF16), w_out.astype(BF16),
                 row2(mlp_norm), w_up.astype(BF16), w_down.astype(BF16), row2(ple_norm),
                 w_ple_gate.astype(BF16), w_ple.astype(BF16), row2(out_gain), tm)


def kernel(x, p, mix_norm, w_in, attn_q_norm, attn_k_norm, ret_decay_logit, ret_norm_gain, w_attn_o, w_ret_o,
           w_out, mlp_norm, w_up, w_down, ple_norm, w_ple_gate, w_ple, final_norm):
    b, s, d = x.shape
    depth = p.shape[0]
    assert b == 1 and depth == 1, "single sequence, single layer"
    tables = _rope_tables(s, ATTN_HEAD_DIM) + _rope_tables(s, RET_HEAD_DIM)
    y = _layer(x[0], p[0, 0], mix_norm[0], w_in[0], attn_q_norm[0], attn_k_norm[0], ret_decay_logit[0],
               ret_norm_gain[0], w_attn_o[0], w_ret_o[0], w_out[0], mlp_norm[0], w_up[0], w_down[0], ple_norm[0],
               w_ple_gate[0], w_ple[0], final_norm, tables, min(ATTN_K_TILE, s), min(ATTN_Q_TILE, s),
               min(RET_CHUNK, s), min(ROW_TILE, s))
    return y[None]
```

```python
import functools

import jax
import jax.numpy as jnp
from jax import lax
from jax.experimental import pallas as pl
from jax.experimental.pallas import tpu as pltpu

F32 = jnp.float32
BF16 = jnp.bfloat16

GRID_W = 64
ATTN_HEAD_DIM = 64
ATTN_HEADS = 8
ATTN_KV_HEADS = 2
ATTN_GROUP = ATTN_HEADS // ATTN_KV_HEADS
RET_HEAD_DIM = 128
RET_HEADS = 4
ROPE_THETA = 10000.0
NORM_EPS = 1e-6
GN_EPS = 1e-5
LOG2E = 1.4426950408889634
SHIFT_LIMIT = 100.0

LANES = 128
MXU_COLS = 256
BF16_SUBLANES = 16
V_ROWS = ATTN_HEAD_DIM + BF16_SUBLANES
VMEM_LIMIT = 56 * 1024 * 1024

ROW_TILE = 256
ATTN_Q_TILE = 512
ATTN_K_TILE = 512
RET_CHUNK = 256


def _dot(a, b):
    return jnp.dot(a, b, preferred_element_type=F32)


def _dot_nt(a, b):
    return lax.dot_general(a, b, (((1,), (1,)), ((), ())), preferred_element_type=F32)


def _dot_tn(a, b):
    return lax.dot_general(a, b, (((0,), (0,)), ((), ())), preferred_element_type=F32)


def _rms(x, gain):
    return x * lax.rsqrt(jnp.mean(x * x, axis=-1, keepdims=True) + NORM_EPS) * gain


def _sigmoid(x):
    return 1.0 / (1.0 + jnp.exp(-x))


def _const_spec(shape):
    return pl.BlockSpec(shape, lambda *_: (0,) * len(shape), pipeline_mode=pl.Buffered(1))


def _inproj_kernel(x_ref, g_ref, w_ref, qg_ref, kg_ref, ca_ref, sa_ref, cr_ref, sr_ref, gm_ref,
                   qT_ref, qn2_ref, k_ref, kn2_ref, vT_ref, qr_ref, kr_ref, vr_ref, gr_ref, sga_ref, sgr_ref):
    tm = x_ref.shape[0]
    h = _rms(x_ref[...], g_ref[...]).astype(BF16)
    lane = lax.broadcasted_iota(jnp.int32, (tm, LANES), 1)
    first_half = (lane % ATTN_HEAD_DIM) < (ATTN_HEAD_DIM // 2)
    ca, sa, cr, sr = ca_ref[...], sa_ref[...], cr_ref[...], sr_ref[...]
    gm = gm_ref[...]

    def proj_pair(c):
        p = _dot(h, w_ref[:, c * LANES:c * LANES + MXU_COLS])
        return p[:, :LANES], p[:, LANES:]

    def head_norm_rope(a, gain):
        sq = a * a
        hi = sq.astype(BF16)
        lo = (sq - hi.astype(F32)).astype(BF16)
        ss = _dot(hi, gm) + _dot(lo, gm)
        n = a * lax.rsqrt(ss * (1.0 / ATTN_HEAD_DIM) + NORM_EPS) * gain
        partner = jnp.where(first_half, pltpu.roll(n, LANES - ATTN_HEAD_DIM // 2, 1),
                            pltpu.roll(n, ATTN_HEAD_DIM // 2, 1))
        return n * ca + partner * sa

    def ret_rope(a):
        return a * cr + pltpu.roll(a, RET_HEAD_DIM // 2, 1) * sr

    col = 0
    zeros = jnp.zeros((ATTN_HEAD_DIM, tm), BF16)
    for c0 in range(0, ATTN_HEADS * ATTN_HEAD_DIM // LANES, 2):
        for c, a in zip((c0, c0 + 1), proj_pair(col + c0)):
            q = head_norm_rope(a, qg_ref[...]) * (ATTN_HEAD_DIM ** -0.5 * LOG2E)
            qt = q.T.astype(BF16)
            for j in range(2):
                head = 2 * c + j
                rows = qt[j * ATTN_HEAD_DIM:(j + 1) * ATTN_HEAD_DIM]
                qT_ref[head, 0:ATTN_HEAD_DIM, :] = rows
                qT_ref[head, ATTN_HEAD_DIM:LANES, :] = zeros
                rf = rows.astype(F32)
                qn2_ref[head] = jnp.sum(rf * rf, axis=0, keepdims=True)
    col += ATTN_HEADS * ATTN_HEAD_DIM // LANES
    k_slab, v_slab = proj_pair(col)
    col += 2
    kf = head_norm_rope(k_slab, kg_ref[...]).astype(BF16).astype(F32)
    ksq = kf * kf
    khi = ksq.astype(BF16)
    kss = _dot(khi, gm) + _dot((ksq - khi.astype(F32)).astype(BF16), gm)
    k_sw, kss_sw = pltpu.roll(kf, ATTN_HEAD_DIM, 1), pltpu.roll(kss, ATTN_HEAD_DIM, 1)
    low = lane < ATTN_HEAD_DIM
    one_lane = jnp.where(lane == ATTN_HEAD_DIM, 1.0, 0.0)
    for g in range(ATTN_KV_HEADS):
        k_ref[g] = jnp.where(low, k_sw if g else kf, one_lane).astype(BF16)
        ss_g = jnp.where(low, kss_sw, kss) if g else jnp.where(low, kss, kss_sw)
        kn2_ref[g, 0] = jnp.broadcast_to(jnp.max(ss_g, axis=0, keepdims=True), (8, LANES))
    vt = v_slab.T.astype(BF16)
    row = lax.broadcasted_iota(jnp.int32, (BF16_SUBLANES, tm), 0)
    ones_row = jnp.where(row == 0, 1.0, 0.0).astype(BF16)
    for g in range(ATTN_KV_HEADS):
        vT_ref[g, 0, 0:ATTN_HEAD_DIM, :] = vt[g * ATTN_HEAD_DIM:(g + 1) * ATTN_HEAD_DIM]
        vT_ref[g, 0, ATTN_HEAD_DIM:V_ROWS, :] = ones_row
    def slabs(first, count, out_ref, fn):
        for c0 in range(0, count, 2):
            for c, a in zip((c0, c0 + 1), proj_pair(first + c0)):
                out_ref[:, c * LANES:(c + 1) * LANES] = fn(a).astype(BF16)

    slabs(col, RET_HEADS, qr_ref, lambda a: ret_rope(a) * (RET_HEAD_DIM ** -0.5))
    slabs(col + RET_HEADS, RET_HEADS, kr_ref, ret_rope)
    slabs(col + 2 * RET_HEADS, RET_HEADS, vr_ref, lambda a: a)
    slabs(col + 3 * RET_HEADS, RET_HEADS, gr_ref, lambda a: a * _sigmoid(a))
    col += 4 * RET_HEADS
    nd = sga_ref.shape[1] // LANES
    slabs(col, nd, sga_ref, _sigmoid)
    slabs(col + nd, nd, sgr_ref, _sigmoid)


def _inproj(x, gain, w, qg, kg, ca, sa, cr, sr, gm, tm):
    s, d = x.shape
    n_in = w.shape[1]
    rw = RET_HEADS * RET_HEAD_DIM
    row = lambda width: pl.BlockSpec((tm, width), lambda i: (i, 0))
    out_shape = (
        jax.ShapeDtypeStruct((ATTN_HEADS, LANES, s), BF16),
        jax.ShapeDtypeStruct((ATTN_HEADS, 1, s), F32),
        jax.ShapeDtypeStruct((ATTN_KV_HEADS, s, LANES), BF16),
        jax.ShapeDtypeStruct((ATTN_KV_HEADS, s // tm, 8, LANES), F32),
        jax.ShapeDtypeStruct((ATTN_KV_HEADS, s // tm, V_ROWS, tm), BF16),
        jax.ShapeDtypeStruct((s, rw), BF16),
        jax.ShapeDtypeStruct((s, rw), BF16),
        jax.ShapeDtypeStruct((s, rw), BF16),
        jax.ShapeDtypeStruct((s, rw), BF16),
        jax.ShapeDtypeStruct((s, d), BF16),
        jax.ShapeDtypeStruct((s, d), BF16),
    )
    out_specs = (
        pl.BlockSpec((ATTN_HEADS, LANES, tm), lambda i: (0, 0, i)),
        pl.BlockSpec((ATTN_HEADS, 1, tm), lambda i: (0, 0, i)),
        pl.BlockSpec((ATTN_KV_HEADS, tm, LANES), lambda i: (0, i, 0)),
        pl.BlockSpec((ATTN_KV_HEADS, 1, 8, LANES), lambda i: (0, i, 0, 0)),
        pl.BlockSpec((ATTN_KV_HEADS, 1, V_ROWS, tm), lambda i: (0, i, 0, 0)),
        row(rw), row(rw), row(rw), row(rw), row(d), row(d),
    )
    in_specs = [
        row(d), _const_spec((1, d)), _const_spec((d, n_in)),
        _const_spec((1, LANES)), _const_spec((1, LANES)),
        row(LANES), row(LANES), row(LANES), row(LANES),
        _const_spec((LANES, LANES)),
    ]
    return pl.pallas_call(
        _inproj_kernel, grid=(s // tm,), in_specs=in_specs, out_specs=out_specs, out_shape=out_shape,
        compiler_params=pltpu.CompilerParams(dimension_semantics=("arbitrary",), vmem_limit_bytes=VMEM_LIMIT),
        name="inproj",
    )(x, gain, w, qg, kg, ca, sa, cr, sr, gm)


def _attn_kernel(qT_ref, qn2_ref, k_ref, vT_ref, kn2_ref, o_ref, qa_scr, acc_scr, p_scr, m_scr, s_scr):
    nk, tk = vT_ref.shape[1], vT_ref.shape[3]
    tq = qT_ref.shape[2]
    acc_scr[...] = jnp.zeros(acc_scr.shape, F32)

    def key_block(j):
        return k_ref[0, pl.ds(pl.multiple_of(j * tk, tk), tk), :]

    kmax2 = jnp.max(jnp.max(kn2_ref[0], axis=0), axis=0, keepdims=True)
    kmax2 = jnp.concatenate([kmax2] * (tq // LANES), axis=1)
    row = lax.broadcasted_iota(jnp.int32, (BF16_SUBLANES, tq), 0)
    bmax = jnp.zeros((1, tq), F32)
    for h in range(ATTN_GROUP):
        b = jnp.sqrt(qn2_ref[h] * kmax2)
        bmax = jnp.maximum(bmax, b)
        qa_scr[h, 0:ATTN_HEAD_DIM, :] = qT_ref[h, 0:ATTN_HEAD_DIM, :]
        qa_scr[h, ATTN_HEAD_DIM:V_ROWS, :] = jnp.where(row == 0, -b, 0.0).astype(BF16)
        qa_scr[h, V_ROWS:LANES, :] = jnp.zeros((LANES - V_ROWS, tq), BF16)
    shift_ok = 2.0 * jnp.max(bmax) <= SHIFT_LIMIT

    @pl.when(shift_ok)
    def _shifted():
        def step(j, buf):
            kb = key_block(j)
            vb = vT_ref[0, jnp.maximum(j - 1, 0)]
            for h in range(ATTN_GROUP):
                p_scr[buf, h] = jnp.exp2(_dot(kb, qa_scr[h])).astype(BF16)
                acc_scr[h] += _dot(vb, p_scr[1 - buf, h])

        p_scr[1] = jnp.zeros(p_scr.shape[1:], BF16)

        unroll = max(u for u in (2, 4, 8) if nk % u == 0)

        def body(i, carry):
            for u in range(unroll):
                step(unroll * i + u, u % 2)
            return carry

        lax.fori_loop(0, nk // unroll, body, 0)
        vb = vT_ref[0, nk - 1]
        for h in range(ATTN_GROUP):
            acc_scr[h] += _dot(vb, p_scr[1, h])

    @pl.when(jnp.logical_not(shift_ok))
    def _online():
        m_scr[...] = jnp.full(m_scr.shape, -jnp.inf, F32)

        def scores(j, buf):
            kb = key_block(j)
            for h in range(ATTN_GROUP):
                s_scr[buf, h] = _dot(kb, qT_ref[h])

        def update(j, buf):
            vb = vT_ref[0, j]
            ps, alphas = [], []
            for h in range(ATTN_GROUP):
                s = s_scr[buf, h]
                m_prev = m_scr[h]
                m_new = jnp.maximum(m_prev, jnp.max(s, axis=0, keepdims=True))
                ps.append(jnp.exp2(s - m_new).astype(BF16))
                alphas.append(jnp.exp2(m_prev - m_new))
                m_scr[h] = m_new
            for h in range(ATTN_GROUP):
                acc_scr[h] = acc_scr[h] * alphas[h] + _dot(vb, ps[h])

        scores(0, 0)

        def body(i, carry):
            j = 2 * i
            scores(j + 1, 1)
            update(j, 0)
            scores(jnp.minimum(j + 2, nk - 1), 0)
            update(j + 1, 1)
            return carry

        lax.fori_loop(0, nk // 2, body, 0)

    outs = []
    for h in range(ATTN_GROUP):
        a = acc_scr[h]
        outs.append(a[0:ATTN_HEAD_DIM] * (1.0 / a[ATTN_HEAD_DIM:ATTN_HEAD_DIM + 1]))
    o_ref[...] = jnp.concatenate(outs, axis=0).T.astype(o_ref.dtype)


def _attention(qT, qn2, k, vT, kn2, tq):
    s = k.shape[1]
    nk, tk = vT.shape[1], vT.shape[3]
    assert nk % 2 == 0 and tq % LANES == 0
    gw = ATTN_GROUP * ATTN_HEAD_DIM
    return pl.pallas_call(
        _attn_kernel,
        grid=(ATTN_KV_HEADS, s // tq),
        in_specs=[
            pl.BlockSpec((ATTN_GROUP, LANES, tq), lambda g, i: (g, 0, i)),
            pl.BlockSpec((ATTN_GROUP, 1, tq), lambda g, i: (g, 0, i)),
            pl.BlockSpec((1, s, LANES), lambda g, i: (g, 0, 0)),
            pl.BlockSpec((1, nk, V_ROWS, tk), lambda g, i: (g, 0, 0, 0)),
            pl.BlockSpec((1, nk, 8, LANES), lambda g, i: (g, 0, 0, 0)),
        ],
        out_specs=pl.BlockSpec((tq, gw), lambda g, i: (i, g)),
        out_shape=jax.ShapeDtypeStruct((s, ATTN_HEADS * ATTN_HEAD_DIM), BF16),
        scratch_shapes=[pltpu.VMEM((ATTN_GROUP, LANES, tq), BF16), pltpu.VMEM((ATTN_GROUP, V_ROWS, tq), F32),
                        pltpu.VMEM((2, ATTN_GROUP, tk, tq), BF16), pltpu.VMEM((ATTN_GROUP, 1, tq), F32),
                        pltpu.VMEM((2, ATTN_GROUP, tk, tq), F32)],
        compiler_params=pltpu.CompilerParams(dimension_semantics=("arbitrary", "arbitrary"),
                                             vmem_limit_bytes=VMEM_LIMIT),
        name="attention",
    )(qT, qn2, k, vT, kn2)


def _retention_kernel(dl_ref, q_ref, k_ref, v_ref, g_ref, gain_ref, o_ref,
                      sf_scr, sb_scr, sball_scr, dmask_scr, kdf_scr, kdb_scr, qdf_scr, qdb_scr, cd_scr):
    ph, n = pl.program_id(0), pl.program_id(1)
    nc = pl.num_programs(1)
    c = q_ref.shape[0]
    hd = RET_HEAD_DIM

    @pl.when((ph == 0) & (n == 0))
    def _init():
        sf_scr[...] = jnp.zeros(sf_scr.shape, F32)
        sb_scr[...] = jnp.zeros(sb_scr.shape, F32)
        pos = lax.broadcasted_iota(jnp.int32, (c, hd), 0).astype(F32)
        ii = lax.broadcasted_iota(jnp.int32, (c, c), 0)
        jj = lax.broadcasted_iota(jnp.int32, (c, c), 1)
        diff = (ii - jj).astype(F32)

        def log_sigmoid(shape, x):
            v = jnp.full(shape, x, F32)
            return -(jnp.maximum(-v, 0.0) + jnp.log(1.0 + jnp.exp(-jnp.abs(v))))

        for h in range(RET_HEADS):
            lgf, lgb = log_sigmoid((c, hd), dl_ref[0, h]), log_sigmoid((c, hd), dl_ref[1, h])
            kdf_scr[h] = jnp.exp(lgf * (c - 1.0 - pos))
            qdf_scr[h] = jnp.exp(lgf * (pos + 1.0))
            kdb_scr[h] = jnp.exp(lgb * pos)
            qdb_scr[h] = jnp.exp(lgb * (c - pos))
            cd_scr[0, h] = jnp.exp(lgf[0:8] * float(c))
            cd_scr[1, h] = jnp.exp(lgb[0:8] * float(c))
            lgf2, lgb2 = log_sigmoid((c, c), dl_ref[0, h]), log_sigmoid((c, c), dl_ref[1, h])
            dmask_scr[h] = jnp.where(diff >= 0, jnp.exp(lgf2 * jnp.maximum(diff, 0.0)),
                                     jnp.exp(lgb2 * jnp.maximum(-diff, 0.0)))

    @pl.when(ph == 0)
    def _backward_states():
        m = nc - 1 - n
        sball_scr[m] = sb_scr[...].astype(BF16)
        for h in range(RET_HEADS):
            sl = slice(h * hd, (h + 1) * hd)
            kd = (k_ref[:, sl].astype(F32) * kdb_scr[h]).astype(BF16)
            sb_scr[h] = sb_scr[h] * cd_scr[1, h, 0:1, :] + _dot_tn(kd, v_ref[:, sl])

    @pl.when(ph == 1)
    def _forward():
        for h in range(RET_HEADS):
            sl = slice(h * hd, (h + 1) * hd)
            q, k, v = q_ref[:, sl], k_ref[:, sl], v_ref[:, sl]
            qf, kf = q.astype(F32), k.astype(F32)
            a = (_dot_nt(q, k) * dmask_scr[h]).astype(BF16)
            y = _dot(a, v)
            y += _dot((qf * qdf_scr[h]).astype(BF16), sf_scr[h].astype(BF16))
            y += _dot((qf * qdb_scr[h]).astype(BF16), sball_scr[n, h])
            sf_scr[h] = sf_scr[h] * cd_scr[0, h, 0:1, :] + _dot_tn((kf * kdf_scr[h]).astype(BF16), v)
            mu = jnp.mean(y, axis=-1, keepdims=True)
            d = y - mu
            var = jnp.mean(d * d, axis=-1, keepdims=True)
            yn = d * lax.rsqrt(var + GN_EPS)
            o_ref[:, sl] = (yn * gain_ref[:, sl] * g_ref[:, sl].astype(F32)).astype(o_ref.dtype)


def _retention(decay_logit, q, k, v, g, gain, c):
    s, w = q.shape
    nc = s // c
    fwd_only = lambda ph, n: (ph * n, 0)
    both = lambda ph, n: (ph * n + (1 - ph) * (nc - 1 - n), 0)
    hd = RET_HEAD_DIM
    return pl.pallas_call(
        _retention_kernel,
        grid=(2, nc),
        in_specs=[
            pl.BlockSpec(memory_space=pltpu.SMEM),
            pl.BlockSpec((c, w), fwd_only), pl.BlockSpec((c, w), both), pl.BlockSpec((c, w), both),
            pl.BlockSpec((c, w), fwd_only), _const_spec((1, w)),
        ],
        out_specs=pl.BlockSpec((c, w), fwd_only),
        out_shape=jax.ShapeDtypeStruct((s, w), BF16),
        scratch_shapes=[
            pltpu.VMEM((RET_HEADS, hd, hd), F32), pltpu.VMEM((RET_HEADS, hd, hd), F32),
            pltpu.VMEM((nc, RET_HEADS, hd, hd), BF16), pltpu.VMEM((RET_HEADS, c, c), F32),
            pltpu.VMEM((RET_HEADS, c, hd), F32), pltpu.VMEM((RET_HEADS, c, hd), F32),
            pltpu.VMEM((RET_HEADS, c, hd), F32), pltpu.VMEM((RET_HEADS, c, hd), F32),
            pltpu.VMEM((2, RET_HEADS, 8, hd), F32),
        ],
        compiler_params=pltpu.CompilerParams(dimension_semantics=("arbitrary", "arbitrary"),
                                             vmem_limit_bytes=VMEM_LIMIT),
        name="retention",
    )(decay_logit, q, k, v, g, gain)


def _post_kernel(x_ref, o_ref, ry_ref, sga_ref, sgr_ref, p_ref, wao_ref, wro_ref, wout_ref, mg_ref,
                 wup_ref, wdn_ref, pg_ref, wpg_ref, wple_ref, fg_ref, out_ref):
    d = x_ref.shape[1]
    merged = (sga_ref[...].astype(F32) * _dot(o_ref[...], wao_ref[...])
              + sgr_ref[...].astype(F32) * _dot(ry_ref[...], wro_ref[...]))
    x1 = x_ref[...] + _dot(merged.astype(BF16), wout_ref[...])
    hm = _rms(x1, mg_ref[...]).astype(BF16)
    x2 = x1
    for c in range(wup_ref.shape[1] // d):
        sl = slice(c * d, (c + 1) * d)
        u = jnp.maximum(_dot(hm, wup_ref[:, sl]), 0.0)
        x2 = x2 + _dot((u * u).astype(BF16), wdn_ref[sl, :])
    gate = _sigmoid(_dot(_rms(x2, pg_ref[...]).astype(BF16), wpg_ref[...]))
    x3 = x2 + gate * _dot(p_ref[...].astype(BF16), wple_ref[...])
    out_ref[...] = _rms(x3, fg_ref[...])


def _post(x, o, ry, sga, sgr, p, wao, wro, wout, mg, wup, wdn, pg, wpg, wple, fg, tm):
    s, d = x.shape
    row = lambda a: pl.BlockSpec((tm, a.shape[1]), lambda i: (i, 0))
    const = lambda a: _const_spec(a.shape)
    return pl.pallas_call(
        _post_kernel, grid=(s // tm,),
        in_specs=[row(x), row(o), row(ry), row(sga), row(sgr), row(p), const(wao), const(wro), const(wout),
                  const(mg), const(wup), const(wdn), const(pg), const(wpg), const(wple), const(fg)],
        out_specs=pl.BlockSpec((tm, d), lambda i: (i, 0)),
        out_shape=jax.ShapeDtypeStruct((s, d), F32),
        compiler_params=pltpu.CompilerParams(dimension_semantics=("arbitrary",), vmem_limit_bytes=VMEM_LIMIT),
        name="post",
    )(x, o, ry, sga, sgr, p, wao, wro, wout, mg, wup, wdn, pg, wpg, wple, fg)


def _rope_tables(seq_len, head_dim):
    n_axis = head_dim // 4
    freqs = ROPE_THETA ** (-jnp.arange(n_axis, dtype=F32) / n_axis)
    t = jnp.arange(seq_len, dtype=jnp.int32)
    row = (t // GRID_W).astype(F32)
    colp = (t % GRID_W).astype(F32)
    ang = jnp.concatenate([row[:, None] * freqs, colp[:, None] * freqs], axis=-1)
    cos, sin = jnp.cos(ang), jnp.sin(ang)
    reps = LANES // head_dim
    return (jnp.tile(jnp.concatenate([cos, cos], axis=-1), (1, reps)),
            jnp.tile(jnp.concatenate([-sin, sin], axis=-1), (1, reps)))


def _layer(x, p, mix_norm, w_in, attn_q_norm, attn_k_norm, ret_decay_logit, ret_norm_gain, w_attn_o, w_ret_o,
           w_out, mlp_norm, w_up, w_down, ple_norm, w_ple_gate, w_ple, out_gain, tables, tk, tq, c, tm):
    ca, sa, cr, sr = tables
    reps = LANES // ATTN_HEAD_DIM
    lane = jnp.arange(LANES) // ATTN_HEAD_DIM
    gm = (lane[:, None] == lane[None, :]).astype(BF16)
    row2 = lambda a: a.reshape(1, -1)
    qT, qn2, k, kn2, vT, qr, kr, vr, gr, sga, sgr = _inproj(
        x, row2(mix_norm), w_in.astype(BF16), row2(jnp.tile(attn_q_norm, reps)), row2(jnp.tile(attn_k_norm, reps)),
        ca, sa, cr, sr, gm, tk)
    o = _attention(qT, qn2, k, vT, kn2, tq)
    ry = _retention(ret_decay_logit, qr, kr, vr, gr, row2(ret_norm_gain), c)
    return _post(x, o, ry, sga, sgr, p, w_attn_o.astype(BF16), w_ret_o.astype(BF16), w_out.astype(BF16),
                 row2(mlp_norm), w_up.astype(BF16), w_down.astype(BF16), row2(ple_norm),
                 w_ple_gate.astype(BF16), w_ple.astype(BF16), row2(out_gain), tm)


def kernel(x, p, mix_norm, w_in, attn_q_norm, attn_k_norm, ret_decay_logit, ret_norm_gain, w_attn_o, w_ret_o,
           w_out, mlp_norm, w_up, w_down, ple_norm, w_ple_gate, w_ple, final_norm):
    b, s, d = x.shape
    depth = p.shape[0]
    assert b == 1 and depth == 1, "single sequence, single layer"
    tables = _rope_tables(s, ATTN_HEAD_DIM) + _rope_tables(s, RET_HEAD_DIM)
    y = _layer(x[0], p[0, 0], mix_norm[0], w_in[0], attn_q_norm[0], attn_k_norm[0], ret_decay_logit[0],
               ret_norm_gain[0], w_attn_o[0], w_ret_o[0], w_out[0], mlp_norm[0], w_up[0], w_down[0], ple_norm[0],
               w_ple_gate[0], w_ple[0], final_norm, tables, min(ATTN_K_TILE, s), min(ATTN_Q_TILE, s),
               min(RET_CHUNK, s), min(ROW_TILE, s))
    return y[None]
```

```python
import functools

import jax
import jax.numpy as jnp
from jax import lax
from jax.experimental import pallas as pl
from jax.experimental.pallas import tpu as pltpu

F32 = jnp.float32
BF16 = jnp.bfloat16

GRID_W = 64
ATTN_HEAD_DIM = 64
ATTN_HEADS = 8
ATTN_KV_HEADS = 2
ATTN_GROUP = ATTN_HEADS // ATTN_KV_HEADS
RET_HEAD_DIM = 128
RET_HEADS = 4
ROPE_THETA = 10000.0
NORM_EPS = 1e-6
GN_EPS = 1e-5
LOG2E = 1.4426950408889634
SHIFT_LIMIT = 100.0

LANES = 128
MXU_COLS = 256
BF16_SUBLANES = 16
Q_BIAS_END = ATTN_HEAD_DIM + BF16_SUBLANES
VMEM_LIMIT = 56 * 1024 * 1024

ROW_TILE = 256
ATTN_Q_TILE = 512
ATTN_K_TILE = 512
RET_CHUNK = 256


def _dot(a, b):
    return jnp.dot(a, b, preferred_element_type=F32)


def _dot_nt(a, b):
    return lax.dot_general(a, b, (((1,), (1,)), ((), ())), preferred_element_type=F32)


def _dot_tn(a, b):
    return lax.dot_general(a, b, (((0,), (0,)), ((), ())), preferred_element_type=F32)


def _rms(x, gain):
    return x * lax.rsqrt(jnp.mean(x * x, axis=-1, keepdims=True) + NORM_EPS) * gain


def _sigmoid(x):
    return 1.0 / (1.0 + jnp.exp(-x))


def _const_spec(shape):
    return pl.BlockSpec(shape, lambda *_: (0,) * len(shape), pipeline_mode=pl.Buffered(1))


def _inproj_kernel(x_ref, g_ref, w_ref, qg_ref, kg_ref, rope_a_refs, rope_r_refs, gm_ref,
                   qT_ref, qn2_ref, k_ref, kn2_ref, vT_ref, qr_ref, kr_ref, vr_ref, gr_ref, sga_ref, sgr_ref):
    tm = x_ref.shape[0]
    h = _rms(x_ref[...], g_ref[...]).astype(BF16)
    lane = lax.broadcasted_iota(jnp.int32, (tm, LANES), 1)
    first_half = (lane % ATTN_HEAD_DIM) < (ATTN_HEAD_DIM // 2)
    gm = gm_ref[...]

    def rope_table(rows_ref, cols_ref, head_dim):
        lane64 = lax.broadcasted_iota(jnp.int32, (GRID_W, LANES), 1)
        row_kind = (lane64 % (head_dim // 2)) < (head_dim // 4)
        cols = cols_ref[...]
        return jnp.concatenate([jnp.where(row_kind, jnp.broadcast_to(rows_ref[r:r + 1, :], (GRID_W, LANES)), cols)
                                for r in range(tm // GRID_W)], axis=0)

    ca, sa = (rope_table(rows, cols, ATTN_HEAD_DIM) for rows, cols in rope_a_refs)
    cr, sr = (rope_table(rows, cols, RET_HEAD_DIM) for rows, cols in rope_r_refs)

    def proj_pair(c):
        p = _dot(h, w_ref[:, c * LANES:c * LANES + MXU_COLS])
        return p[:, :LANES], p[:, LANES:]

    def head_norm_rope(a, gain):
        sq = a * a
        hi = sq.astype(BF16)
        lo = (sq - hi.astype(F32)).astype(BF16)
        ss = _dot(hi, gm) + _dot(lo, gm)
        n = a * lax.rsqrt(ss * (1.0 / ATTN_HEAD_DIM) + NORM_EPS) * gain
        partner = jnp.where(first_half, pltpu.roll(n, LANES - ATTN_HEAD_DIM // 2, 1),
                            pltpu.roll(n, ATTN_HEAD_DIM // 2, 1))
        return n * ca + partner * sa

    def ret_rope(a):
        return a * cr + pltpu.roll(a, RET_HEAD_DIM // 2, 1) * sr

    col = 0
    zeros = jnp.zeros((ATTN_HEAD_DIM, tm), BF16)
    for c0 in range(0, ATTN_HEADS * ATTN_HEAD_DIM // LANES, 2):
        for c, a in zip((c0, c0 + 1), proj_pair(col + c0)):
            q = head_norm_rope(a, qg_ref[...]) * (ATTN_HEAD_DIM ** -0.5 * LOG2E)
            qt = q.T.astype(BF16)
            for j in range(2):
                head = 2 * c + j
                rows = qt[j * ATTN_HEAD_DIM:(j + 1) * ATTN_HEAD_DIM]
                qT_ref[head, 0:ATTN_HEAD_DIM, :] = rows
                qT_ref[head, ATTN_HEAD_DIM:LANES, :] = zeros
                rf = rows.astype(F32)
                qn2_ref[head] = jnp.sum(rf * rf, axis=0, keepdims=True)
    col += ATTN_HEADS * ATTN_HEAD_DIM // LANES
    k_slab, v_slab = proj_pair(col)
    col += 2
    kf = head_norm_rope(k_slab, kg_ref[...]).astype(BF16).astype(F32)
    ksq = kf * kf
    khi = ksq.astype(BF16)
    kss = _dot(khi, gm) + _dot((ksq - khi.astype(F32)).astype(BF16), gm)
    k_sw, kss_sw = pltpu.roll(kf, ATTN_HEAD_DIM, 1), pltpu.roll(kss, ATTN_HEAD_DIM, 1)
    low = lane < ATTN_HEAD_DIM
    one_lane = jnp.where(lane == ATTN_HEAD_DIM, 1.0, 0.0)
    for g in range(ATTN_KV_HEADS):
        k_ref[g] = jnp.where(low, k_sw if g else kf, one_lane).astype(BF16)
        ss_g = jnp.where(low, kss_sw, kss) if g else jnp.where(low, kss, kss_sw)
        kn2_ref[g, 0] = jnp.broadcast_to(jnp.max(ss_g, axis=0, keepdims=True), (8, LANES))
    vt = v_slab.T.astype(BF16)
    for g in range(ATTN_KV_HEADS):
        vT_ref[g, 0] = vt[g * ATTN_HEAD_DIM:(g + 1) * ATTN_HEAD_DIM]
    def slabs(first, count, out_ref, fn):
        for c0 in range(0, count, 2):
            for c, a in zip((c0, c0 + 1), proj_pair(first + c0)):
                out_ref[:, c * LANES:(c + 1) * LANES] = fn(a).astype(BF16)

    slabs(col, RET_HEADS, qr_ref, lambda a: ret_rope(a) * (RET_HEAD_DIM ** -0.5))
    slabs(col + RET_HEADS, RET_HEADS, kr_ref, ret_rope)
    slabs(col + 2 * RET_HEADS, RET_HEADS, vr_ref, lambda a: a)
    slabs(col + 3 * RET_HEADS, RET_HEADS, gr_ref, lambda a: a * _sigmoid(a))
    col += 4 * RET_HEADS
    nd = sga_ref.shape[1] // LANES
    slabs(col, nd, sga_ref, _sigmoid)
    slabs(col + nd, nd, sgr_ref, _sigmoid)


def _inproj(x, gain, w, qg, kg, rope_a, rope_r, gm, tm):
    s, d = x.shape
    n_in = w.shape[1]
    rw = RET_HEADS * RET_HEAD_DIM
    assert tm % GRID_W == 0
    row = lambda width: pl.BlockSpec((tm, width), lambda i: (i, 0))
    rope_spec = ((pl.BlockSpec((tm // GRID_W, LANES), lambda i: (i, 0)), _const_spec((GRID_W, LANES))),) * 2
    out_shape = (
        jax.ShapeDtypeStruct((ATTN_HEADS, LANES, s), BF16),
        jax.ShapeDtypeStruct((ATTN_HEADS, 1, s), F32),
        jax.ShapeDtypeStruct((ATTN_KV_HEADS, s, LANES), BF16),
        jax.ShapeDtypeStruct((ATTN_KV_HEADS, s // tm, 8, LANES), F32),
        jax.ShapeDtypeStruct((ATTN_KV_HEADS, s // tm, ATTN_HEAD_DIM, tm), BF16),
        jax.ShapeDtypeStruct((s, rw), BF16),
        jax.ShapeDtypeStruct((s, rw), BF16),
        jax.ShapeDtypeStruct((s, rw), BF16),
        jax.ShapeDtypeStruct((s, rw), BF16),
        jax.ShapeDtypeStruct((s, d), BF16),
        jax.ShapeDtypeStruct((s, d), BF16),
    )
    out_specs = (
        pl.BlockSpec((ATTN_HEADS, LANES, tm), lambda i: (0, 0, i)),
        pl.BlockSpec((ATTN_HEADS, 1, tm), lambda i: (0, 0, i)),
        pl.BlockSpec((ATTN_KV_HEADS, tm, LANES), lambda i: (0, i, 0)),
        pl.BlockSpec((ATTN_KV_HEADS, 1, 8, LANES), lambda i: (0, i, 0, 0)),
        pl.BlockSpec((ATTN_KV_HEADS, 1, ATTN_HEAD_DIM, tm), lambda i: (0, i, 0, 0)),
        row(rw), row(rw), row(rw), row(rw), row(d), row(d),
    )
    in_specs = [
        row(d), _const_spec((1, d)), _const_spec((d, n_in)),
        _const_spec((1, LANES)), _const_spec((1, LANES)),
        rope_spec, rope_spec,
        _const_spec((LANES, LANES)),
    ]
    return pl.pallas_call(
        _inproj_kernel, grid=(s // tm,), in_specs=in_specs, out_specs=out_specs, out_shape=out_shape,
        compiler_params=pltpu.CompilerParams(dimension_semantics=("arbitrary",), vmem_limit_bytes=VMEM_LIMIT),
        name="inproj",
    )(x, gain, w, qg, kg, rope_a, rope_r, gm)


def _attn_kernel(qT_ref, qn2_ref, k_ref, vT_ref, kn2_ref, o_ref, qa_scr, acc_scr, l_scr, p_scr, m_scr, s_scr):
    nk, tk = vT_ref.shape[1], vT_ref.shape[3]
    tq = qT_ref.shape[2]
    acc_scr[...] = jnp.zeros(acc_scr.shape, F32)
    l_scr[...] = jnp.zeros(l_scr.shape, F32)

    def key_block(j):
        return k_ref[0, pl.ds(pl.multiple_of(j * tk, tk), tk), :]

    def sublane_sums(p):
        return jnp.sum(p.reshape(tk // 8, 8, tq), axis=0)

    kmax2 = jnp.max(jnp.max(kn2_ref[0], axis=0), axis=0, keepdims=True)
    kmax2 = jnp.concatenate([kmax2] * (tq // LANES), axis=1)
    row = lax.broadcasted_iota(jnp.int32, (BF16_SUBLANES, tq), 0)
    bmax = jnp.zeros((1, tq), F32)
    for h in range(ATTN_GROUP):
        b = jnp.sqrt(qn2_ref[h] * kmax2)
        bmax = jnp.maximum(bmax, b)
        qa_scr[h, 0:ATTN_HEAD_DIM, :] = qT_ref[h, 0:ATTN_HEAD_DIM, :]
        qa_scr[h, ATTN_HEAD_DIM:Q_BIAS_END, :] = jnp.where(row == 0, -b, 0.0).astype(BF16)
        qa_scr[h, Q_BIAS_END:LANES, :] = jnp.zeros((LANES - Q_BIAS_END, tq), BF16)
    shift_ok = 2.0 * jnp.max(bmax) <= SHIFT_LIMIT

    @pl.when(shift_ok)
    def _shifted():
        def step(j, buf):
            kb = key_block(j)
            vb = vT_ref[0, jnp.maximum(j - 1, 0)]
            for h in range(ATTN_GROUP):
                p = jnp.exp2(_dot(kb, qa_scr[h]))
                l_scr[h] += sublane_sums(p)
                p_scr[buf, h] = p.astype(BF16)
                acc_scr[h] += _dot(vb, p_scr[1 - buf, h])

        p_scr[1] = jnp.zeros(p_scr.shape[1:], BF16)

        unroll = max(u for u in (2, 4, 8) if nk % u == 0)

        def body(i, carry):
            for u in range(unroll):
                step(unroll * i + u, u % 2)
            return carry

        lax.fori_loop(0, nk // unroll, body, 0)
        vb = vT_ref[0, nk - 1]
        for h in range(ATTN_GROUP):
            acc_scr[h] += _dot(vb, p_scr[1, h])

    @pl.when(jnp.logical_not(shift_ok))
    def _online():
        m_scr[...] = jnp.full(m_scr.shape, -jnp.inf, F32)

        def scores(j, buf):
            kb = key_block(j)
            for h in range(ATTN_GROUP):
                s_scr[buf, h] = _dot(kb, qT_ref[h])

        def update(j, buf):
            vb = vT_ref[0, j]
            ps, alphas = [], []
            for h in range(ATTN_GROUP):
                s = s_scr[buf, h]
                m_prev = m_scr[h]
                m_new = jnp.maximum(m_prev, jnp.max(s, axis=0, keepdims=True))
                p = jnp.exp2(s - m_new)
                alpha = jnp.exp2(m_prev - m_new)
                l_scr[h] = l_scr[h] * alpha + sublane_sums(p)
                ps.append(p.astype(BF16))
                alphas.append(alpha)
                m_scr[h] = m_new
            for h in range(ATTN_GROUP):
                acc_scr[h] = acc_scr[h] * alphas[h] + _dot(vb, ps[h])

        scores(0, 0)

        def body(i, carry):
            j = 2 * i
            scores(j + 1, 1)
            update(j, 0)
            scores(jnp.minimum(j + 2, nk - 1), 0)
            update(j + 1, 1)
            return carry

        lax.fori_loop(0, nk // 2, body, 0)

    outs = [acc_scr[h] * (1.0 / jnp.sum(l_scr[h], axis=0, keepdims=True)) for h in range(ATTN_GROUP)]
    o_ref[...] = jnp.concatenate(outs, axis=0).T.astype(o_ref.dtype)


def _attention(qT, qn2, k, vT, kn2, tq):
    s = k.shape[1]
    nk, tk = vT.shape[1], vT.shape[3]
    assert nk % 2 == 0 and tq % LANES == 0
    gw = ATTN_GROUP * ATTN_HEAD_DIM
    return pl.pallas_call(
        _attn_kernel,
        grid=(ATTN_KV_HEADS, s // tq),
        in_specs=[
            pl.BlockSpec((ATTN_GROUP, LANES, tq), lambda g, i: (g, 0, i)),
            pl.BlockSpec((ATTN_GROUP, 1, tq), lambda g, i: (g, 0, i)),
            pl.BlockSpec((1, s, LANES), lambda g, i: (g, 0, 0)),
            pl.BlockSpec((1, nk, ATTN_HEAD_DIM, tk), lambda g, i: (g, 0, 0, 0)),
            pl.BlockSpec((1, nk, 8, LANES), lambda g, i: (g, 0, 0, 0)),
        ],
        out_specs=pl.BlockSpec((tq, gw), lambda g, i: (i, g)),
        out_shape=jax.ShapeDtypeStruct((s, ATTN_HEADS * ATTN_HEAD_DIM), BF16),
        scratch_shapes=[pltpu.VMEM((ATTN_GROUP, LANES, tq), BF16), pltpu.VMEM((ATTN_GROUP, ATTN_HEAD_DIM, tq), F32),
                        pltpu.VMEM((ATTN_GROUP, 8, tq), F32),
                        pltpu.VMEM((2, ATTN_GROUP, tk, tq), BF16), pltpu.VMEM((ATTN_GROUP, 1, tq), F32),
                        pltpu.VMEM((2, ATTN_GROUP, tk, tq), F32)],
        compiler_params=pltpu.CompilerParams(dimension_semantics=("arbitrary", "arbitrary"),
                                             vmem_limit_bytes=VMEM_LIMIT),
        name="attention",
    )(qT, qn2, k, vT, kn2)


def _retention_kernel(dl_ref, q_ref, k_ref, v_ref, g_ref, gain_ref, o_ref,
                      sf_scr, sb_scr, sball_scr, dmask_scr, kdf_scr, kdb_scr, qdf_scr, qdb_scr, cd_scr):
    ph, n = pl.program_id(0), pl.program_id(1)
    nc = pl.num_programs(1)
    c = q_ref.shape[0]
    hd = RET_HEAD_DIM

    @pl.when((ph == 0) & (n == 0))
    def _init():
        sf_scr[...] = jnp.zeros(sf_scr.shape, F32)
        sb_scr[...] = jnp.zeros(sb_scr.shape, F32)
        pos = lax.broadcasted_iota(jnp.int32, (c, hd), 0).astype(F32)
        ii = lax.broadcasted_iota(jnp.int32, (c, c), 0)
        jj = lax.broadcasted_iota(jnp.int32, (c, c), 1)
        diff = (ii - jj).astype(F32)

        def log_sigmoid(shape, x):
            v = jnp.full(shape, x, F32)
            return -(jnp.maximum(-v, 0.0) + jnp.log(1.0 + jnp.exp(-jnp.abs(v))))

        for h in range(RET_HEADS):
            lgf, lgb = log_sigmoid((c, hd), dl_ref[0, h]), log_sigmoid((c, hd), dl_ref[1, h])
            kdf_scr[h] = jnp.exp(lgf * (c - 1.0 - pos))
            qdf_scr[h] = jnp.exp(lgf * (pos + 1.0))
            kdb_scr[h] = jnp.exp(lgb * pos)
            qdb_scr[h] = jnp.exp(lgb * (c - pos))
            cd_scr[0, h] = jnp.exp(lgf[0:8] * float(c))
            cd_scr[1, h] = jnp.exp(lgb[0:8] * float(c))
            lgf2, lgb2 = log_sigmoid((c, c), dl_ref[0, h]), log_sigmoid((c, c), dl_ref[1, h])
            dmask_scr[h] = jnp.where(diff >= 0, jnp.exp(lgf2 * jnp.maximum(diff, 0.0)),
                                     jnp.exp(lgb2 * jnp.maximum(-diff, 0.0)))

    @pl.when(ph == 0)
    def _backward_states():
        m = nc - 1 - n
        sball_scr[m] = sb_scr[...].astype(BF16)
        for h in range(RET_HEADS):
            sl = slice(h * hd, (h + 1) * hd)
            kd = (k_ref[:, sl].astype(F32) * kdb_scr[h]).astype(BF16)
            sb_scr[h] = sb_scr[h] * cd_scr[1, h, 0:1, :] + _dot_tn(kd, v_ref[:, sl])

    @pl.when(ph == 1)
    def _forward():
        for h in range(RET_HEADS):
            sl = slice(h * hd, (h + 1) * hd)
            q, k, v = q_ref[:, sl], k_ref[:, sl], v_ref[:, sl]
            qf, kf = q.astype(F32), k.astype(F32)
            a = (_dot_nt(q, k) * dmask_scr[h]).astype(BF16)
            y = _dot(a, v)
            y += _dot((qf * qdf_scr[h]).astype(BF16), sf_scr[h].astype(BF16))
            y += _dot((qf * qdb_scr[h]).astype(BF16), sball_scr[n, h])
            sf_scr[h] = sf_scr[h] * cd_scr[0, h, 0:1, :] + _dot_tn((kf * kdf_scr[h]).astype(BF16), v)
            mu = jnp.mean(y, axis=-1, keepdims=True)
            d = y - mu
            var = jnp.mean(d * d, axis=-1, keepdims=True)
            yn = d * lax.rsqrt(var + GN_EPS)
            o_ref[:, sl] = (yn * gain_ref[:, sl] * g_ref[:, sl].astype(F32)).astype(o_ref.dtype)


def _retention(decay_logit, q, k, v, g, gain, c):
    s, w = q.shape
    nc = s // c
    fwd_only = lambda ph, n: (ph * n, 0)
    both = lambda ph, n: (ph * n + (1 - ph) * (nc - 1 - n), 0)
    hd = RET_HEAD_DIM
    return pl.pallas_call(
        _retention_kernel,
        grid=(2, nc),
        in_specs=[
            pl.BlockSpec(memory_space=pltpu.SMEM),
            pl.BlockSpec((c, w), fwd_only), pl.BlockSpec((c, w), both), pl.BlockSpec((c, w), both),
            pl.BlockSpec((c, w), fwd_only), _const_spec((1, w)),
        ],
        out_specs=pl.BlockSpec((c, w), fwd_only),
        out_shape=jax.ShapeDtypeStruct((s, w), BF16),
        scratch_shapes=[
            pltpu.VMEM((RET_HEADS, hd, hd), F32), pltpu.VMEM((RET_HEADS, hd, hd), F32),
            pltpu.VMEM((nc, RET_HEADS, hd, hd), BF16), pltpu.VMEM((RET_HEADS, c, c), F32),
            pltpu.VMEM((RET_HEADS, c, hd), F32), pltpu.VMEM((RET_HEADS, c, hd), F32),
            pltpu.VMEM((RET_HEADS, c, hd), F32), pltpu.VMEM((RET_HEADS, c, hd), F32),
            pltpu.VMEM((2, RET_HEADS, 8, hd), F32),
        ],
        compiler_params=pltpu.CompilerParams(dimension_semantics=("arbitrary", "arbitrary"),
                                             vmem_limit_bytes=VMEM_LIMIT),
        name="retention",
    )(decay_logit, q, k, v, g, gain)


def _post_kernel(x_ref, o_ref, ry_ref, sga_ref, sgr_ref, p_ref, wao_ref, wro_ref, wout_ref, mg_ref,
                 wup_ref, wdn_ref, pg_ref, wpg_ref, wple_ref, fg_ref, out_ref):
    d = x_ref.shape[1]
    merged = (sga_ref[...].astype(F32) * _dot(o_ref[...], wao_ref[...])
              + sgr_ref[...].astype(F32) * _dot(ry_ref[...], wro_ref[...]))
    x1 = x_ref[...] + _dot(merged.astype(BF16), wout_ref[...])
    hm = _rms(x1, mg_ref[...]).astype(BF16)
    x2 = x1
    for c in range(wup_ref.shape[1] // d):
        sl = slice(c * d, (c + 1) * d)
        u = jnp.maximum(_dot(hm, wup_ref[:, sl]), 0.0)
        x2 = x2 + _dot((u * u).astype(BF16), wdn_ref[sl, :])
    gate = _sigmoid(_dot(_rms(x2, pg_ref[...]).astype(BF16), wpg_ref[...]))
    x3 = x2 + gate * _dot(p_ref[...].astype(BF16), wple_ref[...])
    out_ref[...] = _rms(x3, fg_ref[...])


def _post(x, o, ry, sga, sgr, p, wao, wro, wout, mg, wup, wdn, pg, wpg, wple, fg, tm):
    s, d = x.shape
    row = lambda a: pl.BlockSpec((tm, a.shape[1]), lambda i: (i, 0))
    const = lambda a: _const_spec(a.shape)
    return pl.pallas_call(
        _post_kernel, grid=(s // tm,),
        in_specs=[row(x), row(o), row(ry), row(sga), row(sgr), row(p), const(wao), const(wro), const(wout),
                  const(mg), const(wup), const(wdn), const(pg), const(wpg), const(wple), const(fg)],
        out_specs=pl.BlockSpec((tm, d), lambda i: (i, 0)),
        out_shape=jax.ShapeDtypeStruct((s, d), F32),
        compiler_params=pltpu.CompilerParams(dimension_semantics=("arbitrary",), vmem_limit_bytes=VMEM_LIMIT),
        name="post",
    )(x, o, ry, sga, sgr, p, wao, wro, wout, mg, wup, wdn, pg, wpg, wple, fg)


def _rope_tables(seq_len, head_dim):
    n_axis = head_dim // 4
    lane = jnp.arange(LANES)
    freqs = ROPE_THETA ** (-(lane % n_axis).astype(F32) / n_axis)
    sign = jnp.where((lane % head_dim) < head_dim // 2, -1.0, 1.0).astype(F32)
    rows = jnp.arange(seq_len // GRID_W, dtype=F32)[:, None] * freqs
    cols = jnp.arange(GRID_W, dtype=F32)[:, None] * freqs
    return (jnp.cos(rows), jnp.cos(cols)), (jnp.sin(rows) * sign, jnp.sin(cols) * sign)


def _layer(x, p, mix_norm, w_in, attn_q_norm, attn_k_norm, ret_decay_logit, ret_norm_gain, w_attn_o, w_ret_o,
           w_out, mlp_norm, w_up, w_down, ple_norm, w_ple_gate, w_ple, out_gain, rope_a, rope_r, tk, tq, c, tm):
    reps = LANES // ATTN_HEAD_DIM
    lane = jnp.arange(LANES) // ATTN_HEAD_DIM
    gm = (lane[:, None] == lane[None, :]).astype(BF16)
    row2 = lambda a: a.reshape(1, -1)
    qT, qn2, k, kn2, vT, qr, kr, vr, gr, sga, sgr = _inproj(
        x, row2(mix_norm), w_in.astype(BF16), row2(jnp.tile(attn_q_norm, reps)), row2(jnp.tile(attn_k_norm, reps)),
        rope_a, rope_r, gm, tk)
    o = _attention(qT, qn2, k, vT, kn2, tq)
    ry = _retention(ret_decay_logit, qr, kr, vr, gr, row2(ret_norm_gain), c)
    return _post(x, o, ry, sga, sgr, p, w_attn_o.astype(BF16), w_ret_o.astype(BF16), w_out.astype(BF16),
                 row2(mlp_norm), w_up.astype(BF16), w_down.astype(BF16), row2(ple_norm),
                 w_ple_gate.astype(BF16), w_ple.astype(BF16), row2(out_gain), tm)


def kernel(x, p, mix_norm, w_in, attn_q_norm, attn_k_norm, ret_decay_logit, ret_norm_gain, w_attn_o, w_ret_o,
           w_out, mlp_norm, w_up, w_down, ple_norm, w_ple_gate, w_ple, final_norm):
    b, s, d = x.shape
    depth = p.shape[0]
    assert b == 1 and depth == 1, "single sequence, single layer"
    y = _layer(x[0], p[0, 0], mix_norm[0], w_in[0], attn_q_norm[0], attn_k_norm[0], ret_decay_logit[0],
               ret_norm_gain[0], w_attn_o[0], w_ret_o[0], w_out[0], mlp_norm[0], w_up[0], w_down[0], ple_norm[0],
               w_ple_gate[0], w_ple[0], final_norm, _rope_tables(s, ATTN_HEAD_DIM), _rope_tables(s, RET_HEAD_DIM),
               min(ATTN_K_TILE, s), min(ATTN_Q_TILE, s), min(RET_CHUNK, s), min(ROW_TILE, s))
    return y[None]
```

```python
import functools

import jax
import jax.numpy as jnp
from jax import lax
from jax.experimental import pallas as pl
from jax.experimental.pallas import tpu as pltpu

F32 = jnp.float32
BF16 = jnp.bfloat16

GRID_W = 64
ATTN_HEAD_DIM = 64
ATTN_HEADS = 8
ATTN_KV_HEADS = 2
ATTN_GROUP = ATTN_HEADS // ATTN_KV_HEADS
RET_HEAD_DIM = 128
RET_HEADS = 4
ROPE_THETA = 10000.0
NORM_EPS = 1e-6
GN_EPS = 1e-5
LOG2E = 1.4426950408889634
SHIFT_LIMIT = 100.0

LANES = 128
MXU_COLS = 256
BF16_SUBLANES = 16
V_ROWS = ATTN_HEAD_DIM + BF16_SUBLANES
Q_BIAS_END = V_ROWS
VMEM_LIMIT = 56 * 1024 * 1024

ROW_TILE = 512
ATTN_Q_TILE = 512
ATTN_K_TILE = 512
RET_CHUNK = 256
RET_ROWS = 1024


def _dot(a, b):
    return jnp.dot(a, b, preferred_element_type=F32)


def _dot_nt(a, b):
    return lax.dot_general(a, b, (((1,), (1,)), ((), ())), preferred_element_type=F32)


def _dot_tn(a, b):
    return lax.dot_general(a, b, (((0,), (0,)), ((), ())), preferred_element_type=F32)


def _rms(x, gain):
    return x * lax.rsqrt(jnp.mean(x * x, axis=-1, keepdims=True) + NORM_EPS) * gain


def _sigmoid(x):
    return 1.0 / (1.0 + jnp.exp(-x))


def _const_spec(shape):
    return pl.BlockSpec(shape, lambda *_: (0,) * len(shape), pipeline_mode=pl.Buffered(1))


def _inproj_kernel(x_ref, g_ref, w_ref, qg_ref, kg_ref, rope_a_refs, rope_r_refs, gm_ref,
                   qT_ref, qn2_ref, k_ref, kn2_ref, vT_ref, qr_ref, kr_ref, vr_ref, gr_ref, sga_ref, sgr_ref):
    tm = x_ref.shape[0]
    h = _rms(x_ref[...], g_ref[...]).astype(BF16)
    lane = lax.broadcasted_iota(jnp.int32, (tm, LANES), 1)
    first_half = (lane % ATTN_HEAD_DIM) < (ATTN_HEAD_DIM // 2)
    gm = gm_ref[...]

    def rope_table(rows_ref, cols_ref, head_dim):
        lane64 = lax.broadcasted_iota(jnp.int32, (GRID_W, LANES), 1)
        row_kind = (lane64 % (head_dim // 2)) < (head_dim // 4)
        cols = cols_ref[...]
        return jnp.concatenate([jnp.where(row_kind, jnp.broadcast_to(rows_ref[r:r + 1, :], (GRID_W, LANES)), cols)
                                for r in range(tm // GRID_W)], axis=0)

    ca, sa = (rope_table(rows, cols, ATTN_HEAD_DIM) for rows, cols in rope_a_refs)
    cr, sr = (rope_table(rows, cols, RET_HEAD_DIM) for rows, cols in rope_r_refs)

    def proj_pair(c):
        p = _dot(h, w_ref[:, c * LANES:c * LANES + MXU_COLS])
        return p[:, :LANES], p[:, LANES:]

    def head_norm_rope(a, gain):
        sq = a * a
        hi = sq.astype(BF16)
        lo = (sq - hi.astype(F32)).astype(BF16)
        ss = _dot(hi, gm) + _dot(lo, gm)
        n = a * lax.rsqrt(ss * (1.0 / ATTN_HEAD_DIM) + NORM_EPS) * gain
        partner = jnp.where(first_half, pltpu.roll(n, LANES - ATTN_HEAD_DIM // 2, 1),
                            pltpu.roll(n, ATTN_HEAD_DIM // 2, 1))
        return n * ca + partner * sa

    def ret_rope(a):
        return a * cr + pltpu.roll(a, RET_HEAD_DIM // 2, 1) * sr

    col = 0
    zeros = jnp.zeros((ATTN_HEAD_DIM, tm), BF16)
    for c0 in range(0, ATTN_HEADS * ATTN_HEAD_DIM // LANES, 2):
        for c, a in zip((c0, c0 + 1), proj_pair(col + c0)):
            q = head_norm_rope(a, qg_ref[...]) * (ATTN_HEAD_DIM ** -0.5 * LOG2E)
            qt = q.T.astype(BF16)
            for j in range(2):
                head = 2 * c + j
                rows = qt[j * ATTN_HEAD_DIM:(j + 1) * ATTN_HEAD_DIM]
                qT_ref[head, 0:ATTN_HEAD_DIM, :] = rows
                qT_ref[head, ATTN_HEAD_DIM:LANES, :] = zeros
                rf = rows.astype(F32)
                qn2_ref[head] = jnp.sum(rf * rf, axis=0, keepdims=True)
    col += ATTN_HEADS * ATTN_HEAD_DIM // LANES
    k_slab, v_slab = proj_pair(col)
    col += 2
    kf = head_norm_rope(k_slab, kg_ref[...]).astype(BF16).astype(F32)
    ksq = kf * kf
    khi = ksq.astype(BF16)
    kss = _dot(khi, gm) + _dot((ksq - khi.astype(F32)).astype(BF16), gm)
    k_sw, kss_sw = pltpu.roll(kf, ATTN_HEAD_DIM, 1), pltpu.roll(kss, ATTN_HEAD_DIM, 1)
    low = lane < ATTN_HEAD_DIM
    one_lane = jnp.where(lane == ATTN_HEAD_DIM, 1.0, 0.0)
    for g in range(ATTN_KV_HEADS):
        k_ref[g] = jnp.where(low, k_sw if g else kf, one_lane).astype(BF16)
        ss_g = jnp.where(low, kss_sw, kss) if g else jnp.where(low, kss, kss_sw)
        kn2_ref[g, 0] = jnp.broadcast_to(jnp.max(ss_g, axis=0, keepdims=True), (8, LANES))
    vt = v_slab.T.astype(BF16)
    row = lax.broadcasted_iota(jnp.int32, (BF16_SUBLANES, tm), 0)
    ones_row = jnp.where(row == 0, 1.0, 0.0).astype(BF16)
    for g in range(ATTN_KV_HEADS):
        vT_ref[g, 0, 0:ATTN_HEAD_DIM, :] = vt[g * ATTN_HEAD_DIM:(g + 1) * ATTN_HEAD_DIM]
        vT_ref[g, 0, ATTN_HEAD_DIM:V_ROWS, :] = ones_row
    def slabs(first, count, out_ref, fn):
        for c0 in range(0, count, 2):
            for c, a in zip((c0, c0 + 1), proj_pair(first + c0)):
                out_ref[:, c * LANES:(c + 1) * LANES] = fn(a).astype(BF16)

    slabs(col, RET_HEADS, qr_ref, lambda a: ret_rope(a) * (RET_HEAD_DIM ** -0.5))
    slabs(col + RET_HEADS, RET_HEADS, kr_ref, ret_rope)
    slabs(col + 2 * RET_HEADS, RET_HEADS, vr_ref, lambda a: a)
    slabs(col + 3 * RET_HEADS, RET_HEADS, gr_ref, lambda a: a * _sigmoid(a))
    col += 4 * RET_HEADS
    nd = sga_ref.shape[1] // LANES
    slabs(col, nd, sga_ref, _sigmoid)
    slabs(col + nd, nd, sgr_ref, _sigmoid)


def _inproj(x, gain, w, qg, kg, rope_a, rope_r, gm, tm):
    s, d = x.shape
    n_in = w.shape[1]
    rw = RET_HEADS * RET_HEAD_DIM
    assert tm % GRID_W == 0
    row = lambda width: pl.BlockSpec((tm, width), lambda i: (i, 0))
    rope_spec = ((pl.BlockSpec((tm // GRID_W, LANES), lambda i: (i, 0)), _const_spec((GRID_W, LANES))),) * 2
    out_shape = (
        jax.ShapeDtypeStruct((ATTN_HEADS, LANES, s), BF16),
        jax.ShapeDtypeStruct((ATTN_HEADS, 1, s), F32),
        jax.ShapeDtypeStruct((ATTN_KV_HEADS, s, LANES), BF16),
        jax.ShapeDtypeStruct((ATTN_KV_HEADS, s // tm, 8, LANES), F32),
        jax.ShapeDtypeStruct((ATTN_KV_HEADS, s // tm, V_ROWS, tm), BF16),
        jax.ShapeDtypeStruct((s, rw), BF16),
        jax.ShapeDtypeStruct((s, rw), BF16),
        jax.ShapeDtypeStruct((s, rw), BF16),
        jax.ShapeDtypeStruct((s, rw), BF16),
        jax.ShapeDtypeStruct((s, d), BF16),
        jax.ShapeDtypeStruct((s, d), BF16),
    )
    out_specs = (
        pl.BlockSpec((ATTN_HEADS, LANES, tm), lambda i: (0, 0, i)),
        pl.BlockSpec((ATTN_HEADS, 1, tm), lambda i: (0, 0, i)),
        pl.BlockSpec((ATTN_KV_HEADS, tm, LANES), lambda i: (0, i, 0)),
        pl.BlockSpec((ATTN_KV_HEADS, 1, 8, LANES), lambda i: (0, i, 0, 0)),
        pl.BlockSpec((ATTN_KV_HEADS, 1, V_ROWS, tm), lambda i: (0, i, 0, 0)),
        row(rw), row(rw), row(rw), row(rw), row(d), row(d),
    )
    in_specs = [
        row(d), _const_spec((1, d)), _const_spec((d, n_in)),
        _const_spec((1, LANES)), _const_spec((1, LANES)),
        rope_spec, rope_spec,
        _const_spec((LANES, LANES)),
    ]
    return pl.pallas_call(
        _inproj_kernel, grid=(s // tm,), in_specs=in_specs, out_specs=out_specs, out_shape=out_shape,
        compiler_params=pltpu.CompilerParams(dimension_semantics=("arbitrary",), vmem_limit_bytes=VMEM_LIMIT),
        name="inproj",
    )(x, gain, w, qg, kg, rope_a, rope_r, gm)


def _attn_kernel(qT_ref, qn2_ref, k_ref, vT_ref, kn2_ref, o_ref, qa_scr, acc_scr, p_scr, m_scr, s_scr):
    nk, tk = vT_ref.shape[1], vT_ref.shape[3]
    tq = qT_ref.shape[2]
    acc_scr[...] = jnp.zeros(acc_scr.shape, F32)

    def key_block(j):
        return k_ref[0, pl.ds(pl.multiple_of(j * tk, tk), tk), :]

    kmax2 = jnp.max(jnp.max(kn2_ref[0], axis=0), axis=0, keepdims=True)
    kmax2 = jnp.concatenate([kmax2] * (tq // LANES), axis=1)
    row = lax.broadcasted_iota(jnp.int32, (BF16_SUBLANES, tq), 0)
    bmax = jnp.zeros((1, tq), F32)
    for h in range(ATTN_GROUP):
        b = jnp.sqrt(qn2_ref[h] * kmax2)
        bmax = jnp.maximum(bmax, b)
        qa_scr[h, 0:ATTN_HEAD_DIM, :] = qT_ref[h, 0:ATTN_HEAD_DIM, :]
        qa_scr[h, ATTN_HEAD_DIM:Q_BIAS_END, :] = jnp.where(row == 0, -b, 0.0).astype(BF16)
        qa_scr[h, Q_BIAS_END:LANES, :] = jnp.zeros((LANES - Q_BIAS_END, tq), BF16)
    shift_ok = 2.0 * jnp.max(bmax) <= SHIFT_LIMIT

    @pl.when(shift_ok)
    def _shifted():
        def step(j, buf):
            kb = key_block(j)
            vb = vT_ref[0, jnp.maximum(j - 1, 0)]
            for h in range(ATTN_GROUP):
                p_scr[buf, h] = jnp.exp2(_dot(kb, qa_scr[h])).astype(BF16)
                acc_scr[h] += _dot(vb, p_scr[1 - buf, h])

        p_scr[1] = jnp.zeros(p_scr.shape[1:], BF16)

        unroll = max(u for u in (2, 4, 8) if nk % u == 0)

        def body(i, carry):
            for u in range(unroll):
                step(unroll * i + u, u % 2)
            return carry

        lax.fori_loop(0, nk // unroll, body, 0)
        vb = vT_ref[0, nk - 1]
        for h in range(ATTN_GROUP):
            acc_scr[h] += _dot(vb, p_scr[1, h])

    @pl.when(jnp.logical_not(shift_ok))
    def _online():
        m_scr[...] = jnp.full(m_scr.shape, -jnp.inf, F32)

        def scores(j, buf):
            kb = key_block(j)
            for h in range(ATTN_GROUP):
                s_scr[buf, h] = _dot(kb, qT_ref[h])

        def update(j, buf):
            vb = vT_ref[0, j]
            ps, alphas = [], []
            for h in range(ATTN_GROUP):
                s = s_scr[buf, h]
                m_prev = m_scr[h]
                m_new = jnp.maximum(m_prev, jnp.max(s, axis=0, keepdims=True))
                ps.append(jnp.exp2(s - m_new).astype(BF16))
                alphas.append(jnp.exp2(m_prev - m_new))
                m_scr[h] = m_new
            for h in range(ATTN_GROUP):
                acc_scr[h] = acc_scr[h] * alphas[h] + _dot(vb, ps[h])

        scores(0, 0)

        def body(i, carry):
            j = 2 * i
            scores(j + 1, 1)
            update(j, 0)
            scores(jnp.minimum(j + 2, nk - 1), 0)
            update(j + 1, 1)
            return carry

        lax.fori_loop(0, nk // 2, body, 0)

    outs = []
    for h in range(ATTN_GROUP):
        a = acc_scr[h]
        outs.append(a[0:ATTN_HEAD_DIM] * (1.0 / a[ATTN_HEAD_DIM:ATTN_HEAD_DIM + 1]))
    o_ref[...] = jnp.concatenate(outs, axis=0).T.astype(o_ref.dtype)


def _attention(qT, qn2, k, vT, kn2, tq):
    s = k.shape[1]
    nk, tk = vT.shape[1], vT.shape[3]
    assert nk % 2 == 0 and tq % LANES == 0
    gw = ATTN_GROUP * ATTN_HEAD_DIM
    return pl.pallas_call(
        _attn_kernel,
        grid=(ATTN_KV_HEADS, s // tq),
        in_specs=[
            pl.BlockSpec((ATTN_GROUP, LANES, tq), lambda g, i: (g, 0, i)),
            pl.BlockSpec((ATTN_GROUP, 1, tq), lambda g, i: (g, 0, i)),
            pl.BlockSpec((1, s, LANES), lambda g, i: (g, 0, 0)),
            pl.BlockSpec((1, nk, V_ROWS, tk), lambda g, i: (g, 0, 0, 0)),
            pl.BlockSpec((1, nk, 8, LANES), lambda g, i: (g, 0, 0, 0)),
        ],
        out_specs=pl.BlockSpec((tq, gw), lambda g, i: (i, g)),
        out_shape=jax.ShapeDtypeStruct((s, ATTN_HEADS * ATTN_HEAD_DIM), BF16),
        scratch_shapes=[pltpu.VMEM((ATTN_GROUP, LANES, tq), BF16), pltpu.VMEM((ATTN_GROUP, V_ROWS, tq), F32),
                        pltpu.VMEM((2, ATTN_GROUP, tk, tq), BF16), pltpu.VMEM((ATTN_GROUP, 1, tq), F32),
                        pltpu.VMEM((2, ATTN_GROUP, tk, tq), F32)],
        compiler_params=pltpu.CompilerParams(dimension_semantics=("arbitrary", "arbitrary"),
                                             vmem_limit_bytes=VMEM_LIMIT),
        name="attention",
    )(qT, qn2, k, vT, kn2)


def _retention_kernel(dl_ref, q_ref, k_ref, v_ref, g_ref, gain_ref, o_ref,
                      sf_scr, sb_scr, sball_scr, dmask_scr, kdf_scr, kdb_scr, qdf_scr, qdb_scr, cd_scr):
    ph, n = pl.program_id(0), pl.program_id(1)
    nb = pl.num_programs(1)
    c = dmask_scr.shape[1]
    nsub = q_ref.shape[0] // c
    hd = RET_HEAD_DIM

    @pl.when((ph == 0) & (n == 0))
    def _init():
        sf_scr[...] = jnp.zeros(sf_scr.shape, F32)
        sb_scr[...] = jnp.zeros(sb_scr.shape, F32)
        pos = lax.broadcasted_iota(jnp.int32, (c, hd), 0).astype(F32)
        ii = lax.broadcasted_iota(jnp.int32, (c, c), 0)
        jj = lax.broadcasted_iota(jnp.int32, (c, c), 1)
        diff = (ii - jj).astype(F32)

        def log_sigmoid(shape, x):
            v = jnp.full(shape, x, F32)
            return -(jnp.maximum(-v, 0.0) + jnp.log(1.0 + jnp.exp(-jnp.abs(v))))

        for h in range(RET_HEADS):
            lgf, lgb = log_sigmoid((c, hd), dl_ref[0, h]), log_sigmoid((c, hd), dl_ref[1, h])
            kdf_scr[h] = jnp.exp(lgf * (c - 1.0 - pos))
            qdf_scr[h] = jnp.exp(lgf * (pos + 1.0))
            kdb_scr[h] = jnp.exp(lgb * pos)
            qdb_scr[h] = jnp.exp(lgb * (c - pos))
            cd_scr[0, h] = jnp.exp(lgf[0:8] * float(c))
            cd_scr[1, h] = jnp.exp(lgb[0:8] * float(c))
            lgf2, lgb2 = log_sigmoid((c, c), dl_ref[0, h]), log_sigmoid((c, c), dl_ref[1, h])
            dmask_scr[h] = jnp.where(diff >= 0, jnp.exp(lgf2 * jnp.maximum(diff, 0.0)),
                                     jnp.exp(lgb2 * jnp.maximum(-diff, 0.0)))

    @pl.when(ph == 0)
    def _backward_states():
        for sub in reversed(range(nsub)):
            rows = slice(sub * c, (sub + 1) * c)
            m = (nb - 1 - n) * nsub + sub
            sball_scr[m] = sb_scr[...].astype(BF16)
            for h in range(RET_HEADS):
                sl = slice(h * hd, (h + 1) * hd)
                kd = (k_ref[rows, sl].astype(F32) * kdb_scr[h]).astype(BF16)
                sb_scr[h] = sb_scr[h] * cd_scr[1, h, 0:1, :] + _dot_tn(kd, v_ref[rows, sl])

    @pl.when(ph == 1)
    def _forward():
        for sub in range(nsub):
            rows = slice(sub * c, (sub + 1) * c)
            m = n * nsub + sub
            for h in range(RET_HEADS):
                sl = slice(h * hd, (h + 1) * hd)
                q, k, v = q_ref[rows, sl], k_ref[rows, sl], v_ref[rows, sl]
                qf, kf = q.astype(F32), k.astype(F32)
                a = (_dot_nt(q, k) * dmask_scr[h]).astype(BF16)
                y = _dot(a, v)
                y += _dot((qf * qdf_scr[h]).astype(BF16), sf_scr[h].astype(BF16))
                y += _dot((qf * qdb_scr[h]).astype(BF16), sball_scr[m, h])
                sf_scr[h] = sf_scr[h] * cd_scr[0, h, 0:1, :] + _dot_tn((kf * kdf_scr[h]).astype(BF16), v)
                mu = jnp.mean(y, axis=-1, keepdims=True)
                d = y - mu
                var = jnp.mean(d * d, axis=-1, keepdims=True)
                yn = d * lax.rsqrt(var + GN_EPS)
                o_ref[rows, sl] = (yn * gain_ref[:, sl] * g_ref[rows, sl].astype(F32)).astype(o_ref.dtype)


def _retention(decay_logit, q, k, v, g, gain, c, rows):
    s, w = q.shape
    nc, nb = s // c, s // rows
    assert rows % c == 0
    fwd_only = lambda ph, n: (ph * n, 0)
    both = lambda ph, n: (ph * n + (1 - ph) * (nb - 1 - n), 0)
    hd = RET_HEAD_DIM
    return pl.pallas_call(
        _retention_kernel,
        grid=(2, nb),
        in_specs=[
            pl.BlockSpec(memory_space=pltpu.SMEM),
            pl.BlockSpec((rows, w), fwd_only), pl.BlockSpec((rows, w), both), pl.BlockSpec((rows, w), both),
            pl.BlockSpec((rows, w), fwd_only), _const_spec((1, w)),
        ],
        out_specs=pl.BlockSpec((rows, w), fwd_only),
        out_shape=jax.ShapeDtypeStruct((s, w), BF16),
        scratch_shapes=[
            pltpu.VMEM((RET_HEADS, hd, hd), F32), pltpu.VMEM((RET_HEADS, hd, hd), F32),
            pltpu.VMEM((nc, RET_HEADS, hd, hd), BF16), pltpu.VMEM((RET_HEADS, c, c), F32),
            pltpu.VMEM((RET_HEADS, c, hd), F32), pltpu.VMEM((RET_HEADS, c, hd), F32),
            pltpu.VMEM((RET_HEADS, c, hd), F32), pltpu.VMEM((RET_HEADS, c, hd), F32),
            pltpu.VMEM((2, RET_HEADS, 8, hd), F32),
        ],
        compiler_params=pltpu.CompilerParams(dimension_semantics=("arbitrary", "arbitrary"),
                                             vmem_limit_bytes=VMEM_LIMIT),
        name="retention",
    )(decay_logit, q, k, v, g, gain)


def _post_kernel(x_ref, o_ref, ry_ref, sga_ref, sgr_ref, p_ref, wao_ref, wro_ref, wout_ref, mg_ref,
                 wup_ref, wdn_ref, pg_ref, wpg_ref, wple_ref, fg_ref, out_ref):
    d = x_ref.shape[1]
    merged = (sga_ref[...].astype(F32) * _dot(o_ref[...], wao_ref[...])
              + sgr_ref[...].astype(F32) * _dot(ry_ref[...], wro_ref[...]))
    x1 = x_ref[...] + _dot(merged.astype(BF16), wout_ref[...])
    hm = _rms(x1, mg_ref[...]).astype(BF16)
    x2 = x1
    for c in range(wup_ref.shape[1] // d):
        sl = slice(c * d, (c + 1) * d)
        u = jnp.maximum(_dot(hm, wup_ref[:, sl]), 0.0)
        x2 = x2 + _dot((u * u).astype(BF16), wdn_ref[sl, :])
    gate = _sigmoid(_dot(_rms(x2, pg_ref[...]).astype(BF16), wpg_ref[...]))
    x3 = x2 + gate * _dot(p_ref[...].astype(BF16), wple_ref[...])
    out_ref[...] = _rms(x3, fg_ref[...])


def _post(x, o, ry, sga, sgr, p, wao, wro, wout, mg, wup, wdn, pg, wpg, wple, fg, tm):
    s, d = x.shape
    row = lambda a: pl.BlockSpec((tm, a.shape[1]), lambda i: (i, 0))
    const = lambda a: _const_spec(a.shape)
    return pl.pallas_call(
        _post_kernel, grid=(s // tm,),
        in_specs=[row(x), row(o), row(ry), row(sga), row(sgr), row(p), const(wao), const(wro), const(wout),
                  const(mg), const(wup), const(wdn), const(pg), const(wpg), const(wple), const(fg)],
        out_specs=pl.BlockSpec((tm, d), lambda i: (i, 0)),
        out_shape=jax.ShapeDtypeStruct((s, d), F32),
        compiler_params=pltpu.CompilerParams(dimension_semantics=("arbitrary",), vmem_limit_bytes=VMEM_LIMIT),
        name="post",
    )(x, o, ry, sga, sgr, p, wao, wro, wout, mg, wup, wdn, pg, wpg, wple, fg)


def _rope_tables(seq_len, head_dim):
    n_axis = head_dim // 4
    lane = jnp.arange(LANES)
    freqs = ROPE_THETA ** (-(lane % n_axis).astype(F32) / n_axis)
    sign = jnp.where((lane % head_dim) < head_dim // 2, -1.0, 1.0).astype(F32)
    rows = jnp.arange(seq_len // GRID_W, dtype=F32)[:, None] * freqs
    cols = jnp.arange(GRID_W, dtype=F32)[:, None] * freqs
    return (jnp.cos(rows), jnp.cos(cols)), (jnp.sin(rows) * sign, jnp.sin(cols) * sign)


def _layer(x, p, mix_norm, w_in, attn_q_norm, attn_k_norm, ret_decay_logit, ret_norm_gain, w_attn_o, w_ret_o,
           w_out, mlp_norm, w_up, w_down, ple_norm, w_ple_gate, w_ple, out_gain, rope_a, rope_r, tk, tq, c, tm):
    reps = LANES // ATTN_HEAD_DIM
    lane = jnp.arange(LANES) // ATTN_HEAD_DIM
    gm = (lane[:, None] == lane[None, :]).astype(BF16)
    row2 = lambda a: a.reshape(1, -1)
    qT, qn2, k, kn2, vT, qr, kr, vr, gr, sga, sgr = _inproj(
        x, row2(mix_norm), w_in.astype(BF16), row2(jnp.tile(attn_q_norm, reps)), row2(jnp.tile(attn_k_norm, reps)),
        rope_a, rope_r, gm, tk)
    o = _attention(qT, qn2, k, vT, kn2, tq)
    ry = _retention(ret_decay_logit, qr, kr, vr, gr, row2(ret_norm_gain), c, min(RET_ROWS, x.shape[0]))
    return _post(x, o, ry, sga, sgr, p, w_attn_o.astype(BF16), w_ret_o.astype(BF16), w_out.astype(BF16),
                 row2(mlp_norm), w_up.astype(BF16), w_down.astype(BF16), row2(ple_norm),
                 w_ple_gate.astype(BF16), w_ple.astype(BF16), row2(out_gain), tm)


def kernel(x, p, mix_norm, w_in, attn_q_norm, attn_k_norm, ret_decay_logit, ret_norm_gain, w_attn_o, w_ret_o,
           w_out, mlp_norm, w_up, w_down, ple_norm, w_ple_gate, w_ple, final_norm):
    b, s, d = x.shape
    depth = p.shape[0]
    assert b == 1 and depth == 1, "single sequence, single layer"
    y = _layer(x[0], p[0, 0], mix_norm[0], w_in[0], attn_q_norm[0], attn_k_norm[0], ret_decay_logit[0],
               ret_norm_gain[0], w_attn_o[0], w_ret_o[0], w_out[0], mlp_norm[0], w_up[0], w_down[0], ple_norm[0],
               w_ple_gate[0], w_ple[0], final_norm, _rope_tables(s, ATTN_HEAD_DIM), _rope_tables(s, RET_HEAD_DIM),
               min(ATTN_K_TILE, s), min(ATTN_Q_TILE, s), min(RET_CHUNK, s), min(ROW_TILE, s))
    return y[None]
```

```python
import functools

import jax
import jax.numpy as jnp
from jax import lax
from jax.experimental import pallas as pl
from jax.experimental.pallas import tpu as pltpu

F32 = jnp.float32
BF16 = jnp.bfloat16

GRID_W = 64
ATTN_HEAD_DIM = 64
ATTN_HEADS = 8
ATTN_KV_HEADS = 2
ATTN_GROUP = ATTN_HEADS // ATTN_KV_HEADS
RET_HEAD_DIM = 128
RET_HEADS = 4
ROPE_THETA = 10000.0
NORM_EPS = 1e-6
GN_EPS = 1e-5
LOG2E = 1.4426950408889634
SHIFT_LIMIT = 100.0

LANES = 128
MXU_COLS = 256
BF16_SUBLANES = 16
V_ROWS = ATTN_HEAD_DIM + BF16_SUBLANES
Q_BIAS_END = V_ROWS
VMEM_LIMIT = 56 * 1024 * 1024

ROW_TILE = 512
ATTN_Q_TILE = 512
ATTN_K_TILE = 512
RET_CHUNK = 256
RET_ROWS = 1024


def _dot(a, b):
    return jnp.dot(a, b, preferred_element_type=F32)


def _dot_nt(a, b):
    return lax.dot_general(a, b, (((1,), (1,)), ((), ())), preferred_element_type=F32)


def _dot_tn(a, b):
    return lax.dot_general(a, b, (((0,), (0,)), ((), ())), preferred_element_type=F32)


def _rms(x, gain):
    return x * lax.rsqrt(jnp.mean(x * x, axis=-1, keepdims=True) + NORM_EPS) * gain


def _sigmoid(x):
    return 1.0 / (1.0 + jnp.exp(-x))


def _const_spec(shape):
    return pl.BlockSpec(shape, lambda *_: (0,) * len(shape), pipeline_mode=pl.Buffered(1))


def _inproj_kernel(x_ref, g_ref, w_ref, qg_ref, kg_ref, rope_a_refs, rope_r_refs, gm_ref,
                   qT_ref, qn2_ref, k_ref, kn2_ref, vT_ref, qr_ref, kr_ref, vr_ref, gr_ref, sga_ref, sgr_ref):
    tm = x_ref.shape[0]
    h = _rms(x_ref[...], g_ref[...]).astype(BF16)
    lane = lax.broadcasted_iota(jnp.int32, (tm, LANES), 1)
    first_half = (lane % ATTN_HEAD_DIM) < (ATTN_HEAD_DIM // 2)
    gm = gm_ref[...]

    def rope_table(rows_ref, cols_ref, head_dim):
        lane64 = lax.broadcasted_iota(jnp.int32, (GRID_W, LANES), 1)
        row_kind = (lane64 % (head_dim // 2)) < (head_dim // 4)
        cols = cols_ref[...]
        return jnp.concatenate([jnp.where(row_kind, jnp.broadcast_to(rows_ref[r:r + 1, :], (GRID_W, LANES)), cols)
                                for r in range(tm // GRID_W)], axis=0)

    ca, sa = (rope_table(rows, cols, ATTN_HEAD_DIM) for rows, cols in rope_a_refs)
    cr, sr = (rope_table(rows, cols, RET_HEAD_DIM) for rows, cols in rope_r_refs)

    def proj_pair(c):
        p = _dot(h, w_ref[:, c * LANES:c * LANES + MXU_COLS])
        return p[:, :LANES], p[:, LANES:]

    def head_norm_rope(a, gain):
        sq = a * a
        hi = sq.astype(BF16)
        lo = (sq - hi.astype(F32)).astype(BF16)
        ss = _dot(hi, gm) + _dot(lo, gm)
        n = a * lax.rsqrt(ss * (1.0 / ATTN_HEAD_DIM) + NORM_EPS) * gain
        partner = jnp.where(first_half, pltpu.roll(n, LANES - ATTN_HEAD_DIM // 2, 1),
                            pltpu.roll(n, ATTN_HEAD_DIM // 2, 1))
        return n * ca + partner * sa

    def ret_rope(a):
        return a * cr + pltpu.roll(a, RET_HEAD_DIM // 2, 1) * sr

    zeros = jnp.zeros((ATTN_HEAD_DIM, tm), BF16)

    def attn_q(c0, slabs):
        for c, a in zip((c0, c0 + 1), slabs):
            q = head_norm_rope(a, qg_ref[...]) * (ATTN_HEAD_DIM ** -0.5 * LOG2E)
            qt = q.T.astype(BF16)
            for j in range(2):
                head = 2 * c + j
                rows = qt[j * ATTN_HEAD_DIM:(j + 1) * ATTN_HEAD_DIM]
                qT_ref[head, 0:ATTN_HEAD_DIM, :] = rows
                qT_ref[head, ATTN_HEAD_DIM:LANES, :] = zeros
                rf = rows.astype(F32)
                qn2_ref[head] = jnp.sum(rf * rf, axis=0, keepdims=True)

    def attn_kv(_, slabs):
        k_slab, v_slab = slabs
        kf = head_norm_rope(k_slab, kg_ref[...]).astype(BF16).astype(F32)
        ksq = kf * kf
        khi = ksq.astype(BF16)
        kss = _dot(khi, gm) + _dot((ksq - khi.astype(F32)).astype(BF16), gm)
        k_sw, kss_sw = pltpu.roll(kf, ATTN_HEAD_DIM, 1), pltpu.roll(kss, ATTN_HEAD_DIM, 1)
        low = lane < ATTN_HEAD_DIM
        one_lane = jnp.where(lane == ATTN_HEAD_DIM, 1.0, 0.0)
        for g in range(ATTN_KV_HEADS):
            k_ref[g] = jnp.where(low, k_sw if g else kf, one_lane).astype(BF16)
            ss_g = jnp.where(low, kss_sw, kss) if g else jnp.where(low, kss, kss_sw)
            kn2_ref[g, 0] = jnp.broadcast_to(jnp.max(ss_g, axis=0, keepdims=True), (8, LANES))
        vt = v_slab.T.astype(BF16)
        row = lax.broadcasted_iota(jnp.int32, (BF16_SUBLANES, tm), 0)
        ones_row = jnp.where(row == 0, 1.0, 0.0).astype(BF16)
        for g in range(ATTN_KV_HEADS):
            vT_ref[g, 0, 0:ATTN_HEAD_DIM, :] = vt[g * ATTN_HEAD_DIM:(g + 1) * ATTN_HEAD_DIM]
            vT_ref[g, 0, ATTN_HEAD_DIM:V_ROWS, :] = ones_row

    def plain(out_ref, fn):
        def task(c0, slabs):
            for c, a in zip((c0, c0 + 1), slabs):
                out_ref[:, c * LANES:(c + 1) * LANES] = fn(a).astype(BF16)
        return task

    nd = sga_ref.shape[1] // LANES
    sections = [
        (ATTN_HEADS * ATTN_HEAD_DIM // LANES, attn_q),
        (2, attn_kv),
        (RET_HEADS, plain(qr_ref, lambda a: ret_rope(a) * (RET_HEAD_DIM ** -0.5))),
        (RET_HEADS, plain(kr_ref, ret_rope)),
        (RET_HEADS, plain(vr_ref, lambda a: a)),
        (RET_HEADS, plain(gr_ref, lambda a: a * _sigmoid(a))),
        (nd, plain(sga_ref, _sigmoid)),
        (nd, plain(sgr_ref, _sigmoid)),
    ]
    tasks, col = [], 0
    for count, consumer in sections:
        tasks += [(col + c0, c0, consumer) for c0 in range(0, count, 2)]
        col += count
    pending = proj_pair(tasks[0][0])
    for i, (_, c0, consumer) in enumerate(tasks):
        slabs = pending
        if i + 1 < len(tasks):
            pending = proj_pair(tasks[i + 1][0])
        consumer(c0, slabs)


def _inproj(x, gain, w, qg, kg, rope_a, rope_r, gm, tm):
    s, d = x.shape
    n_in = w.shape[1]
    rw = RET_HEADS * RET_HEAD_DIM
    assert tm % GRID_W == 0
    row = lambda width: pl.BlockSpec((tm, width), lambda i: (i, 0))
    rope_spec = ((pl.BlockSpec((tm // GRID_W, LANES), lambda i: (i, 0)), _const_spec((GRID_W, LANES))),) * 2
    out_shape = (
        jax.ShapeDtypeStruct((ATTN_HEADS, LANES, s), BF16),
        jax.ShapeDtypeStruct((ATTN_HEADS, 1, s), F32),
        jax.ShapeDtypeStruct((ATTN_KV_HEADS, s, LANES), BF16),
        jax.ShapeDtypeStruct((ATTN_KV_HEADS, s // tm, 8, LANES), F32),
        jax.ShapeDtypeStruct((ATTN_KV_HEADS, s // tm, V_ROWS, tm), BF16),
        jax.ShapeDtypeStruct((s, rw), BF16),
        jax.ShapeDtypeStruct((s, rw), BF16),
        jax.ShapeDtypeStruct((s, rw), BF16),
        jax.ShapeDtypeStruct((s, rw), BF16),
        jax.ShapeDtypeStruct((s, d), BF16),
        jax.ShapeDtypeStruct((s, d), BF16),
    )
    out_specs = (
        pl.BlockSpec((ATTN_HEADS, LANES, tm), lambda i: (0, 0, i)),
        pl.BlockSpec((ATTN_HEADS, 1, tm), lambda i: (0, 0, i)),
        pl.BlockSpec((ATTN_KV_HEADS, tm, LANES), lambda i: (0, i, 0)),
        pl.BlockSpec((ATTN_KV_HEADS, 1, 8, LANES), lambda i: (0, i, 0, 0)),
        pl.BlockSpec((ATTN_KV_HEADS, 1, V_ROWS, tm), lambda i: (0, i, 0, 0)),
        row(rw), row(rw), row(rw), row(rw), row(d), row(d),
    )
    in_specs = [
        row(d), _const_spec((1, d)), _const_spec((d, n_in)),
        _const_spec((1, LANES)), _const_spec((1, LANES)),
        rope_spec, rope_spec,
        _const_spec((LANES, LANES)),
    ]
    return pl.pallas_call(
        _inproj_kernel, grid=(s // tm,), in_specs=in_specs, out_specs=out_specs, out_shape=out_shape,
        compiler_params=pltpu.CompilerParams(dimension_semantics=("arbitrary",), vmem_limit_bytes=VMEM_LIMIT),
        name="inproj",
    )(x, gain, w, qg, kg, rope_a, rope_r, gm)


def _attn_kernel(qT_ref, qn2_ref, k_ref, vT_ref, kn2_ref, o_ref, qa_scr, acc_scr, p_scr, m_scr, s_scr):
    nk, tk = vT_ref.shape[1], vT_ref.shape[3]
    tq = qT_ref.shape[2]
    acc_scr[...] = jnp.zeros(acc_scr.shape, F32)

    def key_block(j):
        return k_ref[0, pl.ds(pl.multiple_of(j * tk, tk), tk), :]

    kmax2 = jnp.max(jnp.max(kn2_ref[0], axis=0), axis=0, keepdims=True)
    kmax2 = jnp.concatenate([kmax2] * (tq // LANES), axis=1)
    row = lax.broadcasted_iota(jnp.int32, (BF16_SUBLANES, tq), 0)
    bmax = jnp.zeros((1, tq), F32)
    for h in range(ATTN_GROUP):
        b = jnp.sqrt(qn2_ref[h] * kmax2)
        bmax = jnp.maximum(bmax, b)
        qa_scr[h, 0:ATTN_HEAD_DIM, :] = qT_ref[h, 0:ATTN_HEAD_DIM, :]
        qa_scr[h, ATTN_HEAD_DIM:Q_BIAS_END, :] = jnp.where(row == 0, -b, 0.0).astype(BF16)
        qa_scr[h, Q_BIAS_END:LANES, :] = jnp.zeros((LANES - Q_BIAS_END, tq), BF16)
    shift_ok = 2.0 * jnp.max(bmax) <= SHIFT_LIMIT

    @pl.when(shift_ok)
    def _shifted():
        def step(j, buf):
            kb = key_block(j)
            vb = vT_ref[0, jnp.maximum(j - 1, 0)]
            for h in range(ATTN_GROUP):
                p_scr[buf, h] = jnp.exp2(_dot(kb, qa_scr[h])).astype(BF16)
                acc_scr[h] += _dot(vb, p_scr[1 - buf, h])

        p_scr[1] = jnp.zeros(p_scr.shape[1:], BF16)

        unroll = max(u for u in (2, 4, 8) if nk % u == 0)

        def body(i, carry):
            for u in range(unroll):
                step(unroll * i + u, u % 2)
            return carry

        lax.fori_loop(0, nk // unroll, body, 0)
        vb = vT_ref[0, nk - 1]
        for h in range(ATTN_GROUP):
            acc_scr[h] += _dot(vb, p_scr[1, h])

    @pl.when(jnp.logical_not(shift_ok))
    def _online():
        m_scr[...] = jnp.full(m_scr.shape, -jnp.inf, F32)

        def scores(j, buf):
            kb = key_block(j)
            for h in range(ATTN_GROUP):
                s_scr[buf, h] = _dot(kb, qT_ref[h])

        def update(j, buf):
            vb = vT_ref[0, j]
            ps, alphas = [], []
            for h in range(ATTN_GROUP):
                s = s_scr[buf, h]
                m_prev = m_scr[h]
                m_new = jnp.maximum(m_prev, jnp.max(s, axis=0, keepdims=True))
                ps.append(jnp.exp2(s - m_new).astype(BF16))
                alphas.append(jnp.exp2(m_prev - m_new))
                m_scr[h] = m_new
            for h in range(ATTN_GROUP):
                acc_scr[h] = acc_scr[h] * alphas[h] + _dot(vb, ps[h])

        scores(0, 0)

        def body(i, carry):
            j = 2 * i
            scores(j + 1, 1)
            update(j, 0)
            scores(jnp.minimum(j + 2, nk - 1), 0)
            update(j + 1, 1)
            return carry

        lax.fori_loop(0, nk // 2, body, 0)

    outs = []
    for h in range(ATTN_GROUP):
        a = acc_scr[h]
        outs.append(a[0:ATTN_HEAD_DIM] * (1.0 / a[ATTN_HEAD_DIM:ATTN_HEAD_DIM + 1]))
    o_ref[...] = jnp.concatenate(outs, axis=0).T.astype(o_ref.dtype)


def _attention(qT, qn2, k, vT, kn2, tq):
    s = k.shape[1]
    nk, tk = vT.shape[1], vT.shape[3]
    assert nk % 2 == 0 and tq % LANES == 0
    gw = ATTN_GROUP * ATTN_HEAD_DIM
    return pl.pallas_call(
        _attn_kernel,
        grid=(ATTN_KV_HEADS, s // tq),
        in_specs=[
            pl.BlockSpec((ATTN_GROUP, LANES, tq), lambda g, i: (g, 0, i)),
            pl.BlockSpec((ATTN_GROUP, 1, tq), lambda g, i: (g, 0, i)),
            pl.BlockSpec((1, s, LANES), lambda g, i: (g, 0, 0)),
            pl.BlockSpec((1, nk, V_ROWS, tk), lambda g, i: (g, 0, 0, 0)),
            pl.BlockSpec((1, nk, 8, LANES), lambda g, i: (g, 0, 0, 0)),
        ],
        out_specs=pl.BlockSpec((tq, gw), lambda g, i: (i, g)),
        out_shape=jax.ShapeDtypeStruct((s, ATTN_HEADS * ATTN_HEAD_DIM), BF16),
        scratch_shapes=[pltpu.VMEM((ATTN_GROUP, LANES, tq), BF16), pltpu.VMEM((ATTN_GROUP, V_ROWS, tq), F32),
                        pltpu.VMEM((2, ATTN_GROUP, tk, tq), BF16), pltpu.VMEM((ATTN_GROUP, 1, tq), F32),
                        pltpu.VMEM((2, ATTN_GROUP, tk, tq), F32)],
        compiler_params=pltpu.CompilerParams(dimension_semantics=("arbitrary", "arbitrary"),
                                             vmem_limit_bytes=VMEM_LIMIT),
        name="attention",
    )(qT, qn2, k, vT, kn2)


def _retention_kernel(dl_ref, q_ref, k_ref, v_ref, g_ref, gain_ref, o_ref,
                      sf_scr, sb_scr, sball_scr, dmask_scr, kdf_scr, kdb_scr, qdf_scr, qdb_scr, cd_scr):
    ph, n = pl.program_id(0), pl.program_id(1)
    nb = pl.num_programs(1)
    c = dmask_scr.shape[1]
    nsub = q_ref.shape[0] // c
    hd = RET_HEAD_DIM

    @pl.when((ph == 0) & (n == 0))
    def _init():
        sf_scr[...] = jnp.zeros(sf_scr.shape, F32)
        sb_scr[...] = jnp.zeros(sb_scr.shape, F32)
        pos = lax.broadcasted_iota(jnp.int32, (c, hd), 0).astype(F32)
        ii = lax.broadcasted_iota(jnp.int32, (c, c), 0)
        jj = lax.broadcasted_iota(jnp.int32, (c, c), 1)
        diff = (ii - jj).astype(F32)

        def log_sigmoid(shape, x):
            v = jnp.full(shape, x, F32)
            return -(jnp.maximum(-v, 0.0) + jnp.log(1.0 + jnp.exp(-jnp.abs(v))))

        for h in range(RET_HEADS):
            lgf, lgb = log_sigmoid((c, hd), dl_ref[0, h]), log_sigmoid((c, hd), dl_ref[1, h])
            kdf_scr[h] = jnp.exp(lgf * (c - 1.0 - pos))
            qdf_scr[h] = jnp.exp(lgf * (pos + 1.0))
            kdb_scr[h] = jnp.exp(lgb * pos)
            qdb_scr[h] = jnp.exp(lgb * (c - pos))
            cd_scr[0, h] = jnp.exp(lgf[0:8] * float(c))
            cd_scr[1, h] = jnp.exp(lgb[0:8] * float(c))
            lgf2, lgb2 = log_sigmoid((c, c), dl_ref[0, h]), log_sigmoid((c, c), dl_ref[1, h])
            dmask_scr[h] = jnp.where(diff >= 0, jnp.exp(lgf2 * jnp.maximum(diff, 0.0)),
                                     jnp.exp(lgb2 * jnp.maximum(-diff, 0.0)))

    @pl.when(ph == 0)
    def _backward_states():
        for sub in reversed(range(nsub)):
            rows = slice(sub * c, (sub + 1) * c)
            m = (nb - 1 - n) * nsub + sub
            sball_scr[m] = sb_scr[...].astype(BF16)
            for h in range(RET_HEADS):
                sl = slice(h * hd, (h + 1) * hd)
                kd = (k_ref[rows, sl].astype(F32) * kdb_scr[h]).astype(BF16)
                sb_scr[h] = sb_scr[h] * cd_scr[1, h, 0:1, :] + _dot_tn(kd, v_ref[rows, sl])

    @pl.when(ph == 1)
    def _forward():
        for sub in range(nsub):
            rows = slice(sub * c, (sub + 1) * c)
            m = n * nsub + sub
            for h in range(RET_HEADS):
                sl = slice(h * hd, (h + 1) * hd)
                q, k, v = q_ref[rows, sl], k_ref[rows, sl], v_ref[rows, sl]
                qf, kf = q.astype(F32), k.astype(F32)
                a = (_dot_nt(q, k) * dmask_scr[h]).astype(BF16)
                y = _dot(a, v)
                y += _dot((qf * qdf_scr[h]).astype(BF16), sf_scr[h].astype(BF16))
                y += _dot((qf * qdb_scr[h]).astype(BF16), sball_scr[m, h])
                sf_scr[h] = sf_scr[h] * cd_scr[0, h, 0:1, :] + _dot_tn((kf * kdf_scr[h]).astype(BF16), v)
                mu = jnp.mean(y, axis=-1, keepdims=True)
                d = y - mu
                var = jnp.mean(d * d, axis=-1, keepdims=True)
                yn = d * lax.rsqrt(var + GN_EPS)
                o_ref[rows, sl] = (yn * gain_ref[:, sl] * g_ref[rows, sl].astype(F32)).astype(o_ref.dtype)


def _retention(decay_logit, q, k, v, g, gain, c, rows):
    s, w = q.shape
    nc, nb = s // c, s // rows
    assert rows % c == 0
    fwd_only = lambda ph, n: (ph * n, 0)
    both = lambda ph, n: (ph * n + (1 - ph) * (nb - 1 - n), 0)
    hd = RET_HEAD_DIM
    return pl.pallas_call(
        _retention_kernel,
        grid=(2, nb),
        in_specs=[
            pl.BlockSpec(memory_space=pltpu.SMEM),
            pl.BlockSpec((rows, w), fwd_only), pl.BlockSpec((rows, w), both), pl.BlockSpec((rows, w), both),
            pl.BlockSpec((rows, w), fwd_only), _const_spec((1, w)),
        ],
        out_specs=pl.BlockSpec((rows, w), fwd_only),
        out_shape=jax.ShapeDtypeStruct((s, w), BF16),
        scratch_shapes=[
            pltpu.VMEM((RET_HEADS, hd, hd), F32), pltpu.VMEM((RET_HEADS, hd, hd), F32),
            pltpu.VMEM((nc, RET_HEADS, hd, hd), BF16), pltpu.VMEM((RET_HEADS, c, c), F32),
            pltpu.VMEM((RET_HEADS, c, hd), F32), pltpu.VMEM((RET_HEADS, c, hd), F32),
            pltpu.VMEM((RET_HEADS, c, hd), F32), pltpu.VMEM((RET_HEADS, c, hd), F32),
            pltpu.VMEM((2, RET_HEADS, 8, hd), F32),
        ],
        compiler_params=pltpu.CompilerParams(dimension_semantics=("arbitrary", "arbitrary"),
                                             vmem_limit_bytes=VMEM_LIMIT),
        name="retention",
    )(decay_logit, q, k, v, g, gain)


def _post_kernel(x_ref, o_ref, ry_ref, sga_ref, sgr_ref, p_ref, wao_ref, wro_ref, wout_ref, mg_ref,
                 wup_ref, wdn_ref, pg_ref, wpg_ref, wple_ref, fg_ref, out_ref):
    d = x_ref.shape[1]
    merged = (sga_ref[...].astype(F32) * _dot(o_ref[...], wao_ref[...])
              + sgr_ref[...].astype(F32) * _dot(ry_ref[...], wro_ref[...]))
    x1 = x_ref[...] + _dot(merged.astype(BF16), wout_ref[...])
    hm = _rms(x1, mg_ref[...]).astype(BF16)
    x2 = x1
    for c in range(wup_ref.shape[1] // d):
        sl = slice(c * d, (c + 1) * d)
        u = jnp.maximum(_dot(hm, wup_ref[:, sl]), 0.0)
        x2 = x2 + _dot((u * u).astype(BF16), wdn_ref[sl, :])
    gate = _sigmoid(_dot(_rms(x2, pg_ref[...]).astype(BF16), wpg_ref[...]))
    x3 = x2 + gate * _dot(p_ref[...].astype(BF16), wple_ref[...])
    out_ref[...] = _rms(x3, fg_ref[...])


def _post(x, o, ry, sga, sgr, p, wao, wro, wout, mg, wup, wdn, pg, wpg, wple, fg, tm):
    s, d = x.shape
    row = lambda a: pl.BlockSpec((tm, a.shape[1]), lambda i: (i, 0))
    const = lambda a: _const_spec(a.shape)
    return pl.pallas_call(
        _post_kernel, grid=(s // tm,),
        in_specs=[row(x), row(o), row(ry), row(sga), row(sgr), row(p), const(wao), const(wro), const(wout),
                  const(mg), const(wup), const(wdn), const(pg), const(wpg), const(wple), const(fg)],
        out_specs=pl.BlockSpec((tm, d), lambda i: (i, 0)),
        out_shape=jax.ShapeDtypeStruct((s, d), F32),
        compiler_params=pltpu.CompilerParams(dimension_semantics=("arbitrary",), vmem_limit_bytes=VMEM_LIMIT),
        name="post",
    )(x, o, ry, sga, sgr, p, wao, wro, wout, mg, wup, wdn, pg, wpg, wple, fg)


def _rope_tables(seq_len, head_dim):
    n_axis = head_dim // 4
    lane = jnp.arange(LANES)
    freqs = ROPE_THETA ** (-(lane % n_axis).astype(F32) / n_axis)
    sign = jnp.where((lane % head_dim) < head_dim // 2, -1.0, 1.0).astype(F32)
    rows = jnp.arange(seq_len // GRID_W, dtype=F32)[:, None] * freqs
    cols = jnp.arange(GRID_W, dtype=F32)[:, None] * freqs
    return (jnp.cos(rows), jnp.cos(cols)), (jnp.sin(rows) * sign, jnp.sin(cols) * sign)


def _layer(x, p, mix_norm, w_in, attn_q_norm, attn_k_norm, ret_decay_logit, ret_norm_gain, w_attn_o, w_ret_o,
           w_out, mlp_norm, w_up, w_down, ple_norm, w_ple_gate, w_ple, out_gain, rope_a, rope_r, tk, tq, c, tm):
    reps = LANES // ATTN_HEAD_DIM
    lane = jnp.arange(LANES) // ATTN_HEAD_DIM
    gm = (lane[:, None] == lane[None, :]).astype(BF16)
    row2 = lambda a: a.reshape(1, -1)
    qT, qn2, k, kn2, vT, qr, kr, vr, gr, sga, sgr = _inproj(
        x, row2(mix_norm), w_in.astype(BF16), row2(jnp.tile(attn_q_norm, reps)), row2(jnp.tile(attn_k_norm, reps)),
        rope_a, rope_r, gm, tk)
    o = _attention(qT, qn2, k, vT, kn2, tq)
    ry = _retention(ret_decay_logit, qr, kr, vr, gr, row2(ret_norm_gain), c, min(RET_ROWS, x.shape[0]))
    return _post(x, o, ry, sga, sgr, p, w_attn_o.astype(BF16), w_ret_o.astype(BF16), w_out.astype(BF16),
                 row2(mlp_norm), w_up.astype(BF16), w_down.astype(BF16), row2(ple_norm),
                 w_ple_gate.astype(BF16), w_ple.astype(BF16), row2(out_gain), tm)


def kernel(x, p, mix_norm, w_in, attn_q_norm, attn_k_norm, ret_decay_logit, ret_norm_gain, w_attn_o, w_ret_o,
           w_out, mlp_norm, w_up, w_down, ple_norm, w_ple_gate, w_ple, final_norm):
    b, s, d = x.shape
    depth = p.shape[0]
    assert b == 1 and depth == 1, "single sequence, single layer"
    y = _layer(x[0], p[0, 0], mix_norm[0], w_in[0], attn_q_norm[0], attn_k_norm[0], ret_decay_logit[0],
               ret_norm_gain[0], w_attn_o[0], w_ret_o[0], w_out[0], mlp_norm[0], w_up[0], w_down[0], ple_norm[0],
               w_ple_gate[0], w_ple[0], final_norm, _rope_tables(s, ATTN_HEAD_DIM), _rope_tables(s, RET_HEAD_DIM),
               min(ATTN_K_TILE, s), min(ATTN_Q_TILE, s), min(RET_CHUNK, s), min(ROW_TILE, s))
    return y[None]
```

```python
import functools

import jax
import jax.numpy as jnp
from jax import lax
from jax.experimental import pallas as pl
from jax.experimental.pallas import tpu as pltpu

F32 = jnp.float32
BF16 = jnp.bfloat16

GRID_W = 64
ATTN_HEAD_DIM = 64
ATTN_HEADS = 8
ATTN_KV_HEADS = 2
ATTN_GROUP = ATTN_HEADS // ATTN_KV_HEADS
RET_HEAD_DIM = 128
RET_HEADS = 4
ROPE_THETA = 10000.0
NORM_EPS = 1e-6
GN_EPS = 1e-5
LOG2E = 1.4426950408889634
SHIFT_LIMIT = 100.0

LANES = 128
MXU_COLS = 256
BF16_SUBLANES = 16
V_ROWS = ATTN_HEAD_DIM + BF16_SUBLANES
Q_BIAS_END = V_ROWS
VMEM_LIMIT = 56 * 1024 * 1024

ROW_TILE = 512
ATTN_Q_TILE = 512
ATTN_K_TILE = 512
RET_CHUNK = 256
RET_ROWS = 1024


def _dot(a, b):
    return jnp.dot(a, b, preferred_element_type=F32)


def _dot_nt(a, b):
    return lax.dot_general(a, b, (((1,), (1,)), ((), ())), preferred_element_type=F32)


def _dot_tn(a, b):
    return lax.dot_general(a, b, (((0,), (0,)), ((), ())), preferred_element_type=F32)


def _rms(x, gain):
    return x * lax.rsqrt(jnp.mean(x * x, axis=-1, keepdims=True) + NORM_EPS) * gain


def _sigmoid(x):
    return 1.0 / (1.0 + jnp.exp(-x))


def _const_spec(shape):
    return pl.BlockSpec(shape, lambda *_: (0,) * len(shape), pipeline_mode=pl.Buffered(1))


def _inproj_kernel(x_ref, g_ref, w_ref, qg_ref, kg_ref, rope_a_refs, rope_r_refs, gm_ref,
                   qT_ref, qn2_ref, k_ref, kn2_ref, vT_ref, qr_ref, kr_ref, vr_ref, gr_ref, sga_ref, sgr_ref):
    tm = x_ref.shape[0]
    h = _rms(x_ref[...], g_ref[...]).astype(BF16)
    lane = lax.broadcasted_iota(jnp.int32, (tm, LANES), 1)
    first_half = (lane % ATTN_HEAD_DIM) < (ATTN_HEAD_DIM // 2)
    gm = gm_ref[...]

    def rope_table(rows_ref, cols_ref, head_dim):
        lane64 = lax.broadcasted_iota(jnp.int32, (GRID_W, LANES), 1)
        row_kind = (lane64 % (head_dim // 2)) < (head_dim // 4)
        cols = cols_ref[...]
        return jnp.concatenate([jnp.where(row_kind, jnp.broadcast_to(rows_ref[r:r + 1, :], (GRID_W, LANES)), cols)
                                for r in range(tm // GRID_W)], axis=0)

    ca, sa = (rope_table(rows, cols, ATTN_HEAD_DIM) for rows, cols in rope_a_refs)
    cr, sr = (rope_table(rows, cols, RET_HEAD_DIM) for rows, cols in rope_r_refs)

    def proj_pair(c):
        p = _dot(h, w_ref[:, c * LANES:c * LANES + MXU_COLS])
        return p[:, :LANES], p[:, LANES:]

    def head_norm_rope(a, gain):
        sq = a * a
        hi = sq.astype(BF16)
        lo = (sq - hi.astype(F32)).astype(BF16)
        ss = _dot(hi, gm) + _dot(lo, gm)
        n = a * lax.rsqrt(ss * (1.0 / ATTN_HEAD_DIM) + NORM_EPS) * gain
        partner = jnp.where(first_half, pltpu.roll(n, LANES - ATTN_HEAD_DIM // 2, 1),
                            pltpu.roll(n, ATTN_HEAD_DIM // 2, 1))
        return n * ca + partner * sa

    def ret_rope(a):
        return a * cr + pltpu.roll(a, RET_HEAD_DIM // 2, 1) * sr

    zeros = jnp.zeros((ATTN_HEAD_DIM, tm), BF16)

    def attn_q(c0, slabs):
        for c, a in zip((c0, c0 + 1), slabs):
            q = head_norm_rope(a, qg_ref[...]) * (ATTN_HEAD_DIM ** -0.5 * LOG2E)
            qt = q.T.astype(BF16)
            for j in range(2):
                head = 2 * c + j
                rows = qt[j * ATTN_HEAD_DIM:(j + 1) * ATTN_HEAD_DIM]
                qT_ref[head, 0:ATTN_HEAD_DIM, :] = rows
                qT_ref[head, ATTN_HEAD_DIM:LANES, :] = zeros
                rf = rows.astype(F32)
                qn2_ref[head] = jnp.sum(rf * rf, axis=0, keepdims=True)

    def attn_kv(_, slabs):
        k_slab, v_slab = slabs
        kf = head_norm_rope(k_slab, kg_ref[...]).astype(BF16).astype(F32)
        ksq = kf * kf
        khi = ksq.astype(BF16)
        kss = _dot(khi, gm) + _dot((ksq - khi.astype(F32)).astype(BF16), gm)
        k_sw, kss_sw = pltpu.roll(kf, ATTN_HEAD_DIM, 1), pltpu.roll(kss, ATTN_HEAD_DIM, 1)
        low = lane < ATTN_HEAD_DIM
        one_lane = jnp.where(lane == ATTN_HEAD_DIM, 1.0, 0.0)
        for g in range(ATTN_KV_HEADS):
            k_ref[g] = jnp.where(low, k_sw if g else kf, one_lane).astype(BF16)
            ss_g = jnp.where(low, kss_sw, kss) if g else jnp.where(low, kss, kss_sw)
            kn2_ref[g, 0] = jnp.broadcast_to(jnp.max(ss_g, axis=0, keepdims=True), (8, LANES))
        vt = v_slab.T.astype(BF16)
        row = lax.broadcasted_iota(jnp.int32, (BF16_SUBLANES, tm), 0)
        ones_row = jnp.where(row == 0, 1.0, 0.0).astype(BF16)
        for g in range(ATTN_KV_HEADS):
            vT_ref[g, 0, 0:ATTN_HEAD_DIM, :] = vt[g * ATTN_HEAD_DIM:(g + 1) * ATTN_HEAD_DIM]
            vT_ref[g, 0, ATTN_HEAD_DIM:V_ROWS, :] = ones_row

    def plain(out_ref, fn):
        def task(c0, slabs):
            for c, a in zip((c0, c0 + 1), slabs):
                out_ref[:, c * LANES:(c + 1) * LANES] = fn(a).astype(BF16)
        return task

    nd = sga_ref.shape[1] // LANES
    sections = [
        (ATTN_HEADS * ATTN_HEAD_DIM // LANES, attn_q),
        (2, attn_kv),
        (RET_HEADS, plain(qr_ref, lambda a: ret_rope(a) * (RET_HEAD_DIM ** -0.5))),
        (RET_HEADS, plain(kr_ref, ret_rope)),
        (RET_HEADS, plain(vr_ref, lambda a: a)),
        (RET_HEADS, plain(gr_ref, lambda a: a * _sigmoid(a))),
        (nd, plain(sga_ref, _sigmoid)),
        (nd, plain(sgr_ref, _sigmoid)),
    ]
    tasks, col = [], 0
    for count, consumer in sections:
        tasks += [(col + c0, c0, consumer) for c0 in range(0, count, 2)]
        col += count
    pending = proj_pair(tasks[0][0])
    for i, (_, c0, consumer) in enumerate(tasks):
        slabs = pending
        if i + 1 < len(tasks):
            pending = proj_pair(tasks[i + 1][0])
        consumer(c0, slabs)


def _inproj(x, gain, w, qg, kg, rope_a, rope_r, gm, tm):
    s, d = x.shape
    n_in = w.shape[1]
    rw = RET_HEADS * RET_HEAD_DIM
    assert tm % GRID_W == 0
    row = lambda width: pl.BlockSpec((tm, width), lambda i: (i, 0))
    rope_spec = ((pl.BlockSpec((tm // GRID_W, LANES), lambda i: (i, 0)), _const_spec((GRID_W, LANES))),) * 2
    out_shape = (
        jax.ShapeDtypeStruct((ATTN_HEADS, LANES, s), BF16),
        jax.ShapeDtypeStruct((ATTN_HEADS, 1, s), F32),
        jax.ShapeDtypeStruct((ATTN_KV_HEADS, s, LANES), BF16),
        jax.ShapeDtypeStruct((ATTN_KV_HEADS, s // tm, 8, LANES), F32),
        jax.ShapeDtypeStruct((ATTN_KV_HEADS, s // tm, V_ROWS, tm), BF16),
        jax.ShapeDtypeStruct((s, rw), BF16),
        jax.ShapeDtypeStruct((s, rw), BF16),
        jax.ShapeDtypeStruct((s, rw), BF16),
        jax.ShapeDtypeStruct((s, rw), BF16),
        jax.ShapeDtypeStruct((s, d), BF16),
        jax.ShapeDtypeStruct((s, d), BF16),
    )
    out_specs = (
        pl.BlockSpec((ATTN_HEADS, LANES, tm), lambda i: (0, 0, i)),
        pl.BlockSpec((ATTN_HEADS, 1, tm), lambda i: (0, 0, i)),
        pl.BlockSpec((ATTN_KV_HEADS, tm, LANES), lambda i: (0, i, 0)),
        pl.BlockSpec((ATTN_KV_HEADS, 1, 8, LANES), lambda i: (0, i, 0, 0)),
        pl.BlockSpec((ATTN_KV_HEADS, 1, V_ROWS, tm), lambda i: (0, i, 0, 0)),
        row(rw), row(rw), row(rw), row(rw), row(d), row(d),
    )
    in_specs = [
        row(d), _const_spec((1, d)), _const_spec((d, n_in)),
        _const_spec((1, LANES)), _const_spec((1, LANES)),
        rope_spec, rope_spec,
        _const_spec((LANES, LANES)),
    ]
    return pl.pallas_call(
        _inproj_kernel, grid=(s // tm,), in_specs=in_specs, out_specs=out_specs, out_shape=out_shape,
        compiler_params=pltpu.CompilerParams(dimension_semantics=("arbitrary",), vmem_limit_bytes=VMEM_LIMIT),
        name="inproj",
    )(x, gain, w, qg, kg, rope_a, rope_r, gm)


def _attn_kernel(qT_ref, qn2_ref, k_ref, vT_ref, kn2_ref, o_ref, qa_scr, acc_scr, p_scr, m_scr, s_scr):
    nk, tk = vT_ref.shape[1], vT_ref.shape[3]
    tq = qT_ref.shape[2]
    acc_scr[...] = jnp.zeros(acc_scr.shape, F32)

    def key_block(j):
        return k_ref[0, pl.ds(pl.multiple_of(j * tk, tk), tk), :]

    kmax2 = jnp.max(jnp.max(kn2_ref[0], axis=0), axis=0, keepdims=True)
    kmax2 = jnp.concatenate([kmax2] * (tq // LANES), axis=1)
    row = lax.broadcasted_iota(jnp.int32, (BF16_SUBLANES, tq), 0)
    bmax = jnp.zeros((1, tq), F32)
    for h in range(ATTN_GROUP):
        b = jnp.sqrt(qn2_ref[h] * kmax2)
        bmax = jnp.maximum(bmax, b)
        qa_scr[h, 0:ATTN_HEAD_DIM, :] = qT_ref[h, 0:ATTN_HEAD_DIM, :]
        qa_scr[h, ATTN_HEAD_DIM:Q_BIAS_END, :] = jnp.where(row == 0, -b, 0.0).astype(BF16)
        qa_scr[h, Q_BIAS_END:LANES, :] = jnp.zeros((LANES - Q_BIAS_END, tq), BF16)
    shift_ok = 2.0 * jnp.max(bmax) <= SHIFT_LIMIT

    @pl.when(shift_ok)
    def _shifted():
        def step(j, buf):
            kb = key_block(j)
            vb = vT_ref[0, jnp.maximum(j - 1, 0)]
            for h in range(ATTN_GROUP):
                for r in range(0, tk, MXU_COLS):
                    rows = slice(r, r + MXU_COLS)
                    p_scr[buf, h, rows, :] = jnp.exp2(_dot(kb[rows], qa_scr[h])).astype(BF16)
                    acc_scr[h] += _dot(vb[:, rows], p_scr[1 - buf, h, rows, :])

        p_scr[1] = jnp.zeros(p_scr.shape[1:], BF16)

        unroll = max(u for u in (2, 4, 8) if nk % u == 0)

        def body(i, carry):
            for u in range(unroll):
                step(unroll * i + u, u % 2)
            return carry

        lax.fori_loop(0, nk // unroll, body, 0)
        vb = vT_ref[0, nk - 1]
        for h in range(ATTN_GROUP):
            acc_scr[h] += _dot(vb, p_scr[1, h])

    @pl.when(jnp.logical_not(shift_ok))
    def _online():
        m_scr[...] = jnp.full(m_scr.shape, -jnp.inf, F32)

        def scores(j, buf):
            kb = key_block(j)
            for h in range(ATTN_GROUP):
                s_scr[buf, h] = _dot(kb, qT_ref[h])

        def update(j, buf):
            vb = vT_ref[0, j]
            ps, alphas = [], []
            for h in range(ATTN_GROUP):
                s = s_scr[buf, h]
                m_prev = m_scr[h]
                m_new = jnp.maximum(m_prev, jnp.max(s, axis=0, keepdims=True))
                ps.append(jnp.exp2(s - m_new).astype(BF16))
                alphas.append(jnp.exp2(m_prev - m_new))
                m_scr[h] = m_new
            for h in range(ATTN_GROUP):
                acc_scr[h] = acc_scr[h] * alphas[h] + _dot(vb, ps[h])

        scores(0, 0)

        def body(i, carry):
            j = 2 * i
            scores(j + 1, 1)
            update(j, 0)
            scores(jnp.minimum(j + 2, nk - 1), 0)
            update(j + 1, 1)
            return carry

        lax.fori_loop(0, nk // 2, body, 0)

    outs = []
    for h in range(ATTN_GROUP):
        a = acc_scr[h]
        outs.append(a[0:ATTN_HEAD_DIM] * (1.0 / a[ATTN_HEAD_DIM:ATTN_HEAD_DIM + 1]))
    o_ref[...] = jnp.concatenate(outs, axis=0).T.astype(o_ref.dtype)


def _attention(qT, qn2, k, vT, kn2, tq):
    s = k.shape[1]
    nk, tk = vT.shape[1], vT.shape[3]
    assert nk % 2 == 0 and tq % LANES == 0
    gw = ATTN_GROUP * ATTN_HEAD_DIM
    return pl.pallas_call(
        _attn_kernel,
        grid=(ATTN_KV_HEADS, s // tq),
        in_specs=[
            pl.BlockSpec((ATTN_GROUP, LANES, tq), lambda g, i: (g, 0, i)),
            pl.BlockSpec((ATTN_GROUP, 1, tq), lambda g, i: (g, 0, i)),
            pl.BlockSpec((1, s, LANES), lambda g, i: (g, 0, 0)),
            pl.BlockSpec((1, nk, V_ROWS, tk), lambda g, i: (g, 0, 0, 0)),
            pl.BlockSpec((1, nk, 8, LANES), lambda g, i: (g, 0, 0, 0)),
        ],
        out_specs=pl.BlockSpec((tq, gw), lambda g, i: (i, g)),
        out_shape=jax.ShapeDtypeStruct((s, ATTN_HEADS * ATTN_HEAD_DIM), BF16),
        scratch_shapes=[pltpu.VMEM((ATTN_GROUP, LANES, tq), BF16), pltpu.VMEM((ATTN_GROUP, V_ROWS, tq), F32),
                        pltpu.VMEM((2, ATTN_GROUP, tk, tq), BF16), pltpu.VMEM((ATTN_GROUP, 1, tq), F32),
                        pltpu.VMEM((2, ATTN_GROUP, tk, tq), F32)],
        compiler_params=pltpu.CompilerParams(dimension_semantics=("arbitrary", "arbitrary"),
                                             vmem_limit_bytes=VMEM_LIMIT),
        name="attention",
    )(qT, qn2, k, vT, kn2)


def _retention_kernel(dl_ref, q_ref, k_ref, v_ref, g_ref, gain_ref, o_ref,
                      sf_scr, sb_scr, sball_scr, dmask_scr, kdf_scr, kdb_scr, qdf_scr, qdb_scr, cd_scr):
    ph, n = pl.program_id(0), pl.program_id(1)
    nb = pl.num_programs(1)
    c = dmask_scr.shape[1]
    nsub = q_ref.shape[0] // c
    hd = RET_HEAD_DIM

    @pl.when((ph == 0) & (n == 0))
    def _init():
        sf_scr[...] = jnp.zeros(sf_scr.shape, F32)
        sb_scr[...] = jnp.zeros(sb_scr.shape, F32)
        pos = lax.broadcasted_iota(jnp.int32, (c, hd), 0).astype(F32)
        ii = lax.broadcasted_iota(jnp.int32, (c, c), 0)
        jj = lax.broadcasted_iota(jnp.int32, (c, c), 1)
        diff = (ii - jj).astype(F32)

        def log_sigmoid(shape, x):
            v = jnp.full(shape, x, F32)
            return -(jnp.maximum(-v, 0.0) + jnp.log(1.0 + jnp.exp(-jnp.abs(v))))

        for h in range(RET_HEADS):
            lgf, lgb = log_sigmoid((c, hd), dl_ref[0, h]), log_sigmoid((c, hd), dl_ref[1, h])
            kdf_scr[h] = jnp.exp(lgf * (c - 1.0 - pos))
            qdf_scr[h] = jnp.exp(lgf * (pos + 1.0))
            kdb_scr[h] = jnp.exp(lgb * pos)
            qdb_scr[h] = jnp.exp(lgb * (c - pos))
            cd_scr[0, h] = jnp.exp(lgf[0:8] * float(c))
            cd_scr[1, h] = jnp.exp(lgb[0:8] * float(c))
            lgf2, lgb2 = log_sigmoid((c, c), dl_ref[0, h]), log_sigmoid((c, c), dl_ref[1, h])
            dmask_scr[h] = jnp.where(diff >= 0, jnp.exp(lgf2 * jnp.maximum(diff, 0.0)),
                                     jnp.exp(lgb2 * jnp.maximum(-diff, 0.0)))

    @pl.when(ph == 0)
    def _backward_states():
        for sub in reversed(range(nsub)):
            rows = slice(sub * c, (sub + 1) * c)
            m = (nb - 1 - n) * nsub + sub
            sball_scr[m] = sb_scr[...].astype(BF16)
            for h in range(RET_HEADS):
                sl = slice(h * hd, (h + 1) * hd)
                kd = (k_ref[rows, sl].astype(F32) * kdb_scr[h]).astype(BF16)
                sb_scr[h] = sb_scr[h] * cd_scr[1, h, 0:1, :] + _dot_tn(kd, v_ref[rows, sl])

    @pl.when(ph == 1)
    def _forward():
        for sub in range(nsub):
            rows = slice(sub * c, (sub + 1) * c)
            m = n * nsub + sub
            for h in range(RET_HEADS):
                sl = slice(h * hd, (h + 1) * hd)
                q, k, v = q_ref[rows, sl], k_ref[rows, sl], v_ref[rows, sl]
                qf, kf = q.astype(F32), k.astype(F32)
                a = (_dot_nt(q, k) * dmask_scr[h]).astype(BF16)
                y = _dot(a, v)
                y += _dot((qf * qdf_scr[h]).astype(BF16), sf_scr[h].astype(BF16))
                y += _dot((qf * qdb_scr[h]).astype(BF16), sball_scr[m, h])
                sf_scr[h] = sf_scr[h] * cd_scr[0, h, 0:1, :] + _dot_tn((kf * kdf_scr[h]).astype(BF16), v)
                mu = jnp.mean(y, axis=-1, keepdims=True)
                d = y - mu
                var = jnp.mean(d * d, axis=-1, keepdims=True)
                yn = d * lax.rsqrt(var + GN_EPS)
                o_ref[rows, sl] = (yn * gain_ref[:, sl] * g_ref[rows, sl].astype(F32)).astype(o_ref.dtype)


def _retention(decay_logit, q, k, v, g, gain, c, rows):
    s, w = q.shape
    nc, nb = s // c, s // rows
    assert rows % c == 0
    fwd_only = lambda ph, n: (ph * n, 0)
    both = lambda ph, n: (ph * n + (1 - ph) * (nb - 1 - n), 0)
    hd = RET_HEAD_DIM
    return pl.pallas_call(
        _retention_kernel,
        grid=(2, nb),
        in_specs=[
            pl.BlockSpec(memory_space=pltpu.SMEM),
            pl.BlockSpec((rows, w), fwd_only), pl.BlockSpec((rows, w), both), pl.BlockSpec((rows, w), both),
            pl.BlockSpec((rows, w), fwd_only), _const_spec((1, w)),
        ],
        out_specs=pl.BlockSpec((rows, w), fwd_only),
        out_shape=jax.ShapeDtypeStruct((s, w), BF16),
        scratch_shapes=[
            pltpu.VMEM((RET_HEADS, hd, hd), F32), pltpu.VMEM((RET_HEADS, hd, hd), F32),
            pltpu.VMEM((nc, RET_HEADS, hd, hd), BF16), pltpu.VMEM((RET_HEADS, c, c), F32),
            pltpu.VMEM((RET_HEADS, c, hd), F32), pltpu.VMEM((RET_HEADS, c, hd), F32),
            pltpu.VMEM((RET_HEADS, c, hd), F32), pltpu.VMEM((RET_HEADS, c, hd), F32),
            pltpu.VMEM((2, RET_HEADS, 8, hd), F32),
        ],
        compiler_params=pltpu.CompilerParams(dimension_semantics=("arbitrary", "arbitrary"),
                                             vmem_limit_bytes=VMEM_LIMIT),
        name="retention",
    )(decay_logit, q, k, v, g, gain)


def _post_kernel(x_ref, o_ref, ry_ref, sga_ref, sgr_ref, p_ref, wao_ref, wro_ref, wout_ref, mg_ref,
                 wup_ref, wdn_ref, pg_ref, wpg_ref, wple_ref, fg_ref, out_ref):
    d = x_ref.shape[1]
    merged = (sga_ref[...].astype(F32) * _dot(o_ref[...], wao_ref[...])
              + sgr_ref[...].astype(F32) * _dot(ry_ref[...], wro_ref[...]))
    x1 = x_ref[...] + _dot(merged.astype(BF16), wout_ref[...])
    hm = _rms(x1, mg_ref[...]).astype(BF16)
    x2 = x1
    for c in range(wup_ref.shape[1] // d):
        sl = slice(c * d, (c + 1) * d)
        u = jnp.maximum(_dot(hm, wup_ref[:, sl]), 0.0)
        x2 = x2 + _dot((u * u).astype(BF16), wdn_ref[sl, :])
    gate = _sigmoid(_dot(_rms(x2, pg_ref[...]).astype(BF16), wpg_ref[...]))
    x3 = x2 + gate * _dot(p_ref[...].astype(BF16), wple_ref[...])
    out_ref[...] = _rms(x3, fg_ref[...])


def _post(x, o, ry, sga, sgr, p, wao, wro, wout, mg, wup, wdn, pg, wpg, wple, fg, tm):
    s, d = x.shape
    row = lambda a: pl.BlockSpec((tm, a.shape[1]), lambda i: (i, 0))
    const = lambda a: _const_spec(a.shape)
    return pl.pallas_call(
        _post_kernel, grid=(s // tm,),
        in_specs=[row(x), row(o), row(ry), row(sga), row(sgr), row(p), const(wao), const(wro), const(wout),
                  const(mg), const(wup), const(wdn), const(pg), const(wpg), const(wple), const(fg)],
        out_specs=pl.BlockSpec((tm, d), lambda i: (i, 0)),
        out_shape=jax.ShapeDtypeStruct((s, d), F32),
        compiler_params=pltpu.CompilerParams(dimension_semantics=("arbitrary",), vmem_limit_bytes=VMEM_LIMIT),
        name="post",
    )(x, o, ry, sga, sgr, p, wao, wro, wout, mg, wup, wdn, pg, wpg, wple, fg)


def _rope_tables(seq_len, head_dim):
    n_axis = head_dim // 4
    lane = jnp.arange(LANES)
    freqs = ROPE_THETA ** (-(lane % n_axis).astype(F32) / n_axis)
    sign = jnp.where((lane % head_dim) < head_dim // 2, -1.0, 1.0).astype(F32)
    rows = jnp.arange(seq_len // GRID_W, dtype=F32)[:, None] * freqs
    cols = jnp.arange(GRID_W, dtype=F32)[:, None] * freqs
    return (jnp.cos(rows), jnp.cos(cols)), (jnp.sin(rows) * sign, jnp.sin(cols) * sign)


def _layer(x, p, mix_norm, w_in, attn_q_norm, attn_k_norm, ret_decay_logit, ret_norm_gain, w_attn_o, w_ret_o,
           w_out, mlp_norm, w_up, w_down, ple_norm, w_ple_gate, w_ple, out_gain, rope_a, rope_r, tk, tq, c, tm):
    reps = LANES // ATTN_HEAD_DIM
    lane = jnp.arange(LANES) // ATTN_HEAD_DIM
    gm = (lane[:, None] == lane[None, :]).astype(BF16)
    row2 = lambda a: a.reshape(1, -1)
    qT, qn2, k, kn2, vT, qr, kr, vr, gr, sga, sgr = _inproj(
        x, row2(mix_norm), w_in.astype(BF16), row2(jnp.tile(attn_q_norm, reps)), row2(jnp.tile(attn_k_norm, reps)),
        rope_a, rope_r, gm, tk)
    o = _attention(qT, qn2, k, vT, kn2, tq)
    ry = _retention(ret_decay_logit, qr, kr, vr, gr, row2(ret_norm_gain), c, min(RET_ROWS, x.shape[0]))
    return _post(x, o, ry, sga, sgr, p, w_attn_o.astype(BF16), w_ret_o.astype(BF16), w_out.astype(BF16),
                 row2(mlp_norm), w_up.astype(BF16), w_down.astype(BF16), row2(ple_norm),
                 w_ple_gate.astype(BF16), w_ple.astype(BF16), row2(out_gain), tm)


def kernel(x, p, mix_norm, w_in, attn_q_norm, attn_k_norm, ret_decay_logit, ret_norm_gain, w_attn_o, w_ret_o,
           w_out, mlp_norm, w_up, w_down, ple_norm, w_ple_gate, w_ple, final_norm):
    b, s, d = x.shape
    depth = p.shape[0]
    assert b == 1 and depth == 1, "single sequence, single layer"
    y = _layer(x[0], p[0, 0], mix_norm[0], w_in[0], attn_q_norm[0], attn_k_norm[0], ret_decay_logit[0],
               ret_norm_gain[0], w_attn_o[0], w_ret_o[0], w_out[0], mlp_norm[0], w_up[0], w_down[0], ple_norm[0],
               w_ple_gate[0], w_ple[0], final_norm, _rope_tables(s, ATTN_HEAD_DIM), _rope_tables(s, RET_HEAD_DIM),
               min(ATTN_K_TILE, s), min(ATTN_Q_TILE, s), min(RET_CHUNK, s), min(ROW_TILE, s))
    return y[None]
```

```python
import functools

import jax
import jax.numpy as jnp
from jax import lax
from jax.experimental import pallas as pl
from jax.experimental.pallas import tpu as pltpu

F32 = jnp.float32
BF16 = jnp.bfloat16

GRID_W = 64
ATTN_HEAD_DIM = 64
ATTN_HEADS = 8
ATTN_KV_HEADS = 2
ATTN_GROUP = ATTN_HEADS // ATTN_KV_HEADS
RET_HEAD_DIM = 128
RET_HEADS = 4
ROPE_THETA = 10000.0
NORM_EPS = 1e-6
GN_EPS = 1e-5
LOG2E = 1.4426950408889634
SHIFT_LIMIT = 100.0

LANES = 128
MXU_COLS = 256
BF16_SUBLANES = 16
Q_BIAS_END = ATTN_HEAD_DIM + BF16_SUBLANES
VMEM_LIMIT = 56 * 1024 * 1024

ROW_TILE = 512
ATTN_Q_TILE = 512
ATTN_K_TILE = 512
RET_CHUNK = 256
RET_ROWS = 1024


def _dot(a, b):
    return jnp.dot(a, b, preferred_element_type=F32)


def _dot_nt(a, b):
    return lax.dot_general(a, b, (((1,), (1,)), ((), ())), preferred_element_type=F32)


def _dot_tn(a, b):
    return lax.dot_general(a, b, (((0,), (0,)), ((), ())), preferred_element_type=F32)


def _rms(x, gain):
    return x * lax.rsqrt(jnp.mean(x * x, axis=-1, keepdims=True) + NORM_EPS) * gain


def _sigmoid(x):
    return 1.0 / (1.0 + jnp.exp(-x))


def _const_spec(shape):
    return pl.BlockSpec(shape, lambda *_: (0,) * len(shape), pipeline_mode=pl.Buffered(1))


def _inproj_kernel(x_ref, g_ref, w_ref, qg_ref, kg_ref, rope_a_refs, rope_r_refs, gm_ref,
                   qT_ref, qn2_ref, k_ref, kn2_ref, vT_ref, qr_ref, kr_ref, vr_ref, gr_ref, sga_ref, sgr_ref):
    tm = x_ref.shape[0]
    h = _rms(x_ref[...], g_ref[...]).astype(BF16)
    lane = lax.broadcasted_iota(jnp.int32, (tm, LANES), 1)
    first_half = (lane % ATTN_HEAD_DIM) < (ATTN_HEAD_DIM // 2)
    gm = gm_ref[...]

    def rope_table(rows_ref, cols_ref, head_dim):
        lane64 = lax.broadcasted_iota(jnp.int32, (GRID_W, LANES), 1)
        row_kind = (lane64 % (head_dim // 2)) < (head_dim // 4)
        cols = cols_ref[...]
        return jnp.concatenate([jnp.where(row_kind, jnp.broadcast_to(rows_ref[r:r + 1, :], (GRID_W, LANES)), cols)
                                for r in range(tm // GRID_W)], axis=0)

    ca, sa = (rope_table(rows, cols, ATTN_HEAD_DIM) for rows, cols in rope_a_refs)
    cr, sr = (rope_table(rows, cols, RET_HEAD_DIM) for rows, cols in rope_r_refs)

    def proj_pair(c):
        p = _dot(h, w_ref[:, c * LANES:c * LANES + MXU_COLS])
        return p[:, :LANES], p[:, LANES:]

    def head_norm_rope(a, gain):
        sq = a * a
        hi = sq.astype(BF16)
        lo = (sq - hi.astype(F32)).astype(BF16)
        ss = _dot(hi, gm) + _dot(lo, gm)
        n = a * lax.rsqrt(ss * (1.0 / ATTN_HEAD_DIM) + NORM_EPS) * gain
        partner = jnp.where(first_half, pltpu.roll(n, LANES - ATTN_HEAD_DIM // 2, 1),
                            pltpu.roll(n, ATTN_HEAD_DIM // 2, 1))
        return n * ca + partner * sa

    def ret_rope(a):
        return a * cr + pltpu.roll(a, RET_HEAD_DIM // 2, 1) * sr

    zeros = jnp.zeros((ATTN_HEAD_DIM, tm), BF16)

    def attn_q(c0, slabs):
        for c, a in zip((c0, c0 + 1), slabs):
            q = head_norm_rope(a, qg_ref[...]) * (ATTN_HEAD_DIM ** -0.5 * LOG2E)
            qt = q.T.astype(BF16)
            for j in range(2):
                head = 2 * c + j
                rows = qt[j * ATTN_HEAD_DIM:(j + 1) * ATTN_HEAD_DIM]
                qT_ref[head, 0:ATTN_HEAD_DIM, :] = rows
                qT_ref[head, ATTN_HEAD_DIM:LANES, :] = zeros
                rf = rows.astype(F32)
                qn2_ref[head] = jnp.sum(rf * rf, axis=0, keepdims=True)

    def attn_kv(_, slabs):
        k_slab, v_slab = slabs
        kf = head_norm_rope(k_slab, kg_ref[...]).astype(BF16).astype(F32)
        ksq = kf * kf
        khi = ksq.astype(BF16)
        kss = _dot(khi, gm) + _dot((ksq - khi.astype(F32)).astype(BF16), gm)
        k_sw, kss_sw = pltpu.roll(kf, ATTN_HEAD_DIM, 1), pltpu.roll(kss, ATTN_HEAD_DIM, 1)
        low = lane < ATTN_HEAD_DIM
        one_lane = jnp.where(lane == ATTN_HEAD_DIM, 1.0, 0.0)
        for g in range(ATTN_KV_HEADS):
            k_ref[g] = jnp.where(low, k_sw if g else kf, one_lane).astype(BF16)
            ss_g = jnp.where(low, kss_sw, kss) if g else jnp.where(low, kss, kss_sw)
            kn2_ref[g, 0] = jnp.broadcast_to(jnp.max(ss_g, axis=0, keepdims=True), (8, LANES))
        vt = v_slab.T.astype(BF16)
        for g in range(ATTN_KV_HEADS):
            vT_ref[g, 0] = vt[g * ATTN_HEAD_DIM:(g + 1) * ATTN_HEAD_DIM]

    def plain(out_ref, fn):
        def task(c0, slabs):
            for c, a in zip((c0, c0 + 1), slabs):
                out_ref[:, c * LANES:(c + 1) * LANES] = fn(a).astype(BF16)
        return task

    nd = sga_ref.shape[1] // LANES
    sections = [
        (ATTN_HEADS * ATTN_HEAD_DIM // LANES, attn_q),
        (2, attn_kv),
        (RET_HEADS, plain(qr_ref, lambda a: ret_rope(a) * (RET_HEAD_DIM ** -0.5))),
        (RET_HEADS, plain(kr_ref, ret_rope)),
        (RET_HEADS, plain(vr_ref, lambda a: a)),
        (RET_HEADS, plain(gr_ref, lambda a: a * _sigmoid(a))),
        (nd, plain(sga_ref, _sigmoid)),
        (nd, plain(sgr_ref, _sigmoid)),
    ]
    tasks, col = [], 0
    for count, consumer in sections:
        tasks += [(col + c0, c0, consumer) for c0 in range(0, count, 2)]
        col += count
    pending = proj_pair(tasks[0][0])
    for i, (_, c0, consumer) in enumerate(tasks):
        slabs = pending
        if i + 1 < len(tasks):
            pending = proj_pair(tasks[i + 1][0])
        consumer(c0, slabs)


def _inproj(x, gain, w, qg, kg, rope_a, rope_r, gm, tm):
    s, d = x.shape
    n_in = w.shape[1]
    rw = RET_HEADS * RET_HEAD_DIM
    assert tm % GRID_W == 0
    row = lambda width: pl.BlockSpec((tm, width), lambda i: (i, 0))
    rope_spec = ((pl.BlockSpec((tm // GRID_W, LANES), lambda i: (i, 0)), _const_spec((GRID_W, LANES))),) * 2
    out_shape = (
        jax.ShapeDtypeStruct((ATTN_HEADS, LANES, s), BF16),
        jax.ShapeDtypeStruct((ATTN_HEADS, 1, s), F32),
        jax.ShapeDtypeStruct((ATTN_KV_HEADS, s, LANES), BF16),
        jax.ShapeDtypeStruct((ATTN_KV_HEADS, s // tm, 8, LANES), F32),
        jax.ShapeDtypeStruct((ATTN_KV_HEADS, s // tm, ATTN_HEAD_DIM, tm), BF16),
        jax.ShapeDtypeStruct((s, rw), BF16),
        jax.ShapeDtypeStruct((s, rw), BF16),
        jax.ShapeDtypeStruct((s, rw), BF16),
        jax.ShapeDtypeStruct((s, rw), BF16),
        jax.ShapeDtypeStruct((s, d), BF16),
        jax.ShapeDtypeStruct((s, d), BF16),
    )
    out_specs = (
        pl.BlockSpec((ATTN_HEADS, LANES, tm), lambda i: (0, 0, i)),
        pl.BlockSpec((ATTN_HEADS, 1, tm), lambda i: (0, 0, i)),
        pl.BlockSpec((ATTN_KV_HEADS, tm, LANES), lambda i: (0, i, 0)),
        pl.BlockSpec((ATTN_KV_HEADS, 1, 8, LANES), lambda i: (0, i, 0, 0)),
        pl.BlockSpec((ATTN_KV_HEADS, 1, ATTN_HEAD_DIM, tm), lambda i: (0, i, 0, 0)),
        row(rw), row(rw), row(rw), row(rw), row(d), row(d),
    )
    in_specs = [
        row(d), _const_spec((1, d)), _const_spec((d, n_in)),
        _const_spec((1, LANES)), _const_spec((1, LANES)),
        rope_spec, rope_spec,
        _const_spec((LANES, LANES)),
    ]
    return pl.pallas_call(
        _inproj_kernel, grid=(s // tm,), in_specs=in_specs, out_specs=out_specs, out_shape=out_shape,
        compiler_params=pltpu.CompilerParams(dimension_semantics=("arbitrary",), vmem_limit_bytes=VMEM_LIMIT),
        name="inproj",
    )(x, gain, w, qg, kg, rope_a, rope_r, gm)


def _attn_kernel(qT_ref, qn2_ref, k_ref, vT_ref, kn2_ref, o_ref, qa_scr, acc_scr, l_scr, p_scr, m_scr, s_scr):
    nk, tk = vT_ref.shape[1], vT_ref.shape[3]
    tq = qT_ref.shape[2]
    acc_scr[...] = jnp.zeros(acc_scr.shape, F32)
    l_scr[...] = jnp.zeros(l_scr.shape, F32)

    def key_block(j):
        return k_ref[0, pl.ds(pl.multiple_of(j * tk, tk), tk), :]

    def sublane_sums(p):
        return jnp.sum(p.reshape(p.shape[0] // 8, 8, tq), axis=0)

    kmax2 = jnp.max(jnp.max(kn2_ref[0], axis=0), axis=0, keepdims=True)
    kmax2 = jnp.concatenate([kmax2] * (tq // LANES), axis=1)
    row = lax.broadcasted_iota(jnp.int32, (BF16_SUBLANES, tq), 0)
    bmax = jnp.zeros((1, tq), F32)
    for h in range(ATTN_GROUP):
        b = jnp.sqrt(qn2_ref[h] * kmax2)
        bmax = jnp.maximum(bmax, b)
        qa_scr[h, 0:ATTN_HEAD_DIM, :] = qT_ref[h, 0:ATTN_HEAD_DIM, :]
        qa_scr[h, ATTN_HEAD_DIM:Q_BIAS_END, :] = jnp.where(row == 0, -b, 0.0).astype(BF16)
        qa_scr[h, Q_BIAS_END:LANES, :] = jnp.zeros((LANES - Q_BIAS_END, tq), BF16)
    shift_ok = 2.0 * jnp.max(bmax) <= SHIFT_LIMIT

    @pl.when(shift_ok)
    def _shifted():
        def step(j, buf):
            kb = key_block(j)
            vb = vT_ref[0, jnp.maximum(j - 1, 0)]
            for h in range(ATTN_GROUP):
                for r in range(0, tk, MXU_COLS):
                    rows = slice(r, r + MXU_COLS)
                    p = jnp.exp2(_dot(kb[rows], qa_scr[h]))
                    l_scr[h] += sublane_sums(p)
                    p_scr[buf, h, rows, :] = p.astype(BF16)
                    acc_scr[h] += _dot(vb[:, rows], p_scr[1 - buf, h, rows, :])

        p_scr[1] = jnp.zeros(p_scr.shape[1:], BF16)

        unroll = max(u for u in (2, 4, 8) if nk % u == 0)

        def body(i, carry):
            for u in range(unroll):
                step(unroll * i + u, u % 2)
            return carry

        lax.fori_loop(0, nk // unroll, body, 0)
        vb = vT_ref[0, nk - 1]
        for h in range(ATTN_GROUP):
            acc_scr[h] += _dot(vb, p_scr[1, h])

    @pl.when(jnp.logical_not(shift_ok))
    def _online():
        m_scr[...] = jnp.full(m_scr.shape, -jnp.inf, F32)

        def scores(j, buf):
            kb = key_block(j)
            for h in range(ATTN_GROUP):
                s_scr[buf, h] = _dot(kb, qT_ref[h])

        def update(j, buf):
            vb = vT_ref[0, j]
            ps, alphas = [], []
            for h in range(ATTN_GROUP):
                s = s_scr[buf, h]
                m_prev = m_scr[h]
                m_new = jnp.maximum(m_prev, jnp.max(s, axis=0, keepdims=True))
                p = jnp.exp2(s - m_new)
                alpha = jnp.exp2(m_prev - m_new)
                l_scr[h] = l_scr[h] * alpha + sublane_sums(p)
                ps.append(p.astype(BF16))
                alphas.append(alpha)
                m_scr[h] = m_new
            for h in range(ATTN_GROUP):
                acc_scr[h] = acc_scr[h] * alphas[h] + _dot(vb, ps[h])

        scores(0, 0)

        def body(i, carry):
            j = 2 * i
            scores(j + 1, 1)
            update(j, 0)
            scores(jnp.minimum(j + 2, nk - 1), 0)
            update(j + 1, 1)
            return carry

        lax.fori_loop(0, nk // 2, body, 0)

    outs = [acc_scr[h] * (1.0 / jnp.sum(l_scr[h], axis=0, keepdims=True)) for h in range(ATTN_GROUP)]
    o_ref[...] = jnp.concatenate(outs, axis=0).T.astype(o_ref.dtype)


def _attention(qT, qn2, k, vT, kn2, tq):
    s = k.shape[1]
    nk, tk = vT.shape[1], vT.shape[3]
    assert nk % 2 == 0 and tq % LANES == 0
    gw = ATTN_GROUP * ATTN_HEAD_DIM
    return pl.pallas_call(
        _attn_kernel,
        grid=(ATTN_KV_HEADS, s // tq),
        in_specs=[
            pl.BlockSpec((ATTN_GROUP, LANES, tq), lambda g, i: (g, 0, i)),
            pl.BlockSpec((ATTN_GROUP, 1, tq), lambda g, i: (g, 0, i)),
            pl.BlockSpec((1, s, LANES), lambda g, i: (g, 0, 0)),
            pl.BlockSpec((1, nk, ATTN_HEAD_DIM, tk), lambda g, i: (g, 0, 0, 0)),
            pl.BlockSpec((1, nk, 8, LANES), lambda g, i: (g, 0, 0, 0)),
        ],
        out_specs=pl.BlockSpec((tq, gw), lambda g, i: (i, g)),
        out_shape=jax.ShapeDtypeStruct((s, ATTN_HEADS * ATTN_HEAD_DIM), BF16),
        scratch_shapes=[pltpu.VMEM((ATTN_GROUP, LANES, tq), BF16), pltpu.VMEM((ATTN_GROUP, ATTN_HEAD_DIM, tq), F32),
                        pltpu.VMEM((ATTN_GROUP, 8, tq), F32),
                        pltpu.VMEM((2, ATTN_GROUP, tk, tq), BF16), pltpu.VMEM((ATTN_GROUP, 1, tq), F32),
                        pltpu.VMEM((2, ATTN_GROUP, tk, tq), F32)],
        compiler_params=pltpu.CompilerParams(dimension_semantics=("arbitrary", "arbitrary"),
                                             vmem_limit_bytes=VMEM_LIMIT),
        name="attention",
    )(qT, qn2, k, vT, kn2)


def _retention_kernel(dl_ref, q_ref, k_ref, v_ref, g_ref, gain_ref, o_ref,
                      sf_scr, sb_scr, sball_scr, dmask_scr, kdf_scr, kdb_scr, qdf_scr, qdb_scr, cd_scr):
    ph, n = pl.program_id(0), pl.program_id(1)
    nb = pl.num_programs(1)
    c = dmask_scr.shape[1]
    nsub = q_ref.shape[0] // c
    hd = RET_HEAD_DIM

    @pl.when((ph == 0) & (n == 0))
    def _init():
        sf_scr[...] = jnp.zeros(sf_scr.shape, F32)
        sb_scr[...] = jnp.zeros(sb_scr.shape, F32)
        pos = lax.broadcasted_iota(jnp.int32, (c, hd), 0).astype(F32)
        ii = lax.broadcasted_iota(jnp.int32, (c, c), 0)
        jj = lax.broadcasted_iota(jnp.int32, (c, c), 1)
        diff = (ii - jj).astype(F32)

        def log_sigmoid(shape, x):
            v = jnp.full(shape, x, F32)
            return -(jnp.maximum(-v, 0.0) + jnp.log(1.0 + jnp.exp(-jnp.abs(v))))

        for h in range(RET_HEADS):
            lgf, lgb = log_sigmoid((c, hd), dl_ref[0, h]), log_sigmoid((c, hd), dl_ref[1, h])
            kdf_scr[h] = jnp.exp(lgf * (c - 1.0 - pos))
            qdf_scr[h] = jnp.exp(lgf * (pos + 1.0))
            kdb_scr[h] = jnp.exp(lgb * pos)
            qdb_scr[h] = jnp.exp(lgb * (c - pos))
            cd_scr[0, h] = jnp.exp(lgf[0:8] * float(c))
            cd_scr[1, h] = jnp.exp(lgb[0:8] * float(c))
            lgf2, lgb2 = log_sigmoid((c, c), dl_ref[0, h]), log_sigmoid((c, c), dl_ref[1, h])
            dmask_scr[h] = jnp.where(diff >= 0, jnp.exp(lgf2 * jnp.maximum(diff, 0.0)),
                                     jnp.exp(lgb2 * jnp.maximum(-diff, 0.0)))

    @pl.when(ph == 0)
    def _backward_states():
        for sub in reversed(range(nsub)):
            rows = slice(sub * c, (sub + 1) * c)
            m = (nb - 1 - n) * nsub + sub
            sball_scr[m] = sb_scr[...].astype(BF16)
            for h in range(RET_HEADS):
                sl = slice(h * hd, (h + 1) * hd)
                kd = (k_ref[rows, sl].astype(F32) * kdb_scr[h]).astype(BF16)
                sb_scr[h] = sb_scr[h] * cd_scr[1, h, 0:1, :] + _dot_tn(kd, v_ref[rows, sl])

    @pl.when(ph == 1)
    def _forward():
        for sub in range(nsub):
            rows = slice(sub * c, (sub + 1) * c)
            m = n * nsub + sub
            for h in range(RET_HEADS):
                sl = slice(h * hd, (h + 1) * hd)
                q, k, v = q_ref[rows, sl], k_ref[rows, sl], v_ref[rows, sl]
                qf, kf = q.astype(F32), k.astype(F32)
                a = (_dot_nt(q, k) * dmask_scr[h]).astype(BF16)
                y = _dot(a, v)
                y += _dot((qf * qdf_scr[h]).astype(BF16), sf_scr[h].astype(BF16))
                y += _dot((qf * qdb_scr[h]).astype(BF16), sball_scr[m, h])
                sf_scr[h] = sf_scr[h] * cd_scr[0, h, 0:1, :] + _dot_tn((kf * kdf_scr[h]).astype(BF16), v)
                mu = jnp.mean(y, axis=-1, keepdims=True)
                d = y - mu
                var = jnp.mean(d * d, axis=-1, keepdims=True)
                yn = d * lax.rsqrt(var + GN_EPS)
                o_ref[rows, sl] = (yn * gain_ref[:, sl] * g_ref[rows, sl].astype(F32)).astype(o_ref.dtype)


def _retention(decay_logit, q, k, v, g, gain, c, rows):
    s, w = q.shape
    nc, nb = s // c, s // rows
    assert rows % c == 0
    fwd_only = lambda ph, n: (ph * n, 0)
    both = lambda ph, n: (ph * n + (1 - ph) * (nb - 1 - n), 0)
    hd = RET_HEAD_DIM
    return pl.pallas_call(
        _retention_kernel,
        grid=(2, nb),
        in_specs=[
            pl.BlockSpec(memory_space=pltpu.SMEM),
            pl.BlockSpec((rows, w), fwd_only), pl.BlockSpec((rows, w), both), pl.BlockSpec((rows, w), both),
            pl.BlockSpec((rows, w), fwd_only), _const_spec((1, w)),
        ],
        out_specs=pl.BlockSpec((rows, w), fwd_only),
        out_shape=jax.ShapeDtypeStruct((s, w), BF16),
        scratch_shapes=[
            pltpu.VMEM((RET_HEADS, hd, hd), F32), pltpu.VMEM((RET_HEADS, hd, hd), F32),
            pltpu.VMEM((nc, RET_HEADS, hd, hd), BF16), pltpu.VMEM((RET_HEADS, c, c), F32),
            pltpu.VMEM((RET_HEADS, c, hd), F32), pltpu.VMEM((RET_HEADS, c, hd), F32),
            pltpu.VMEM((RET_HEADS, c, hd), F32), pltpu.VMEM((RET_HEADS, c, hd), F32),
            pltpu.VMEM((2, RET_HEADS, 8, hd), F32),
        ],
        compiler_params=pltpu.CompilerParams(dimension_semantics=("arbitrary", "arbitrary"),
                                             vmem_limit_bytes=VMEM_LIMIT),
        name="retention",
    )(decay_logit, q, k, v, g, gain)


def _post_kernel(x_ref, o_ref, ry_ref, sga_ref, sgr_ref, p_ref, wao_ref, wro_ref, wout_ref, mg_ref,
                 wup_ref, wdn_ref, pg_ref, wpg_ref, wple_ref, fg_ref, out_ref):
    d = x_ref.shape[1]
    merged = (sga_ref[...].astype(F32) * _dot(o_ref[...], wao_ref[...])
              + sgr_ref[...].astype(F32) * _dot(ry_ref[...], wro_ref[...]))
    x1 = x_ref[...] + _dot(merged.astype(BF16), wout_ref[...])
    hm = _rms(x1, mg_ref[...]).astype(BF16)
    x2 = x1
    for c in range(wup_ref.shape[1] // d):
        sl = slice(c * d, (c + 1) * d)
        u = jnp.maximum(_dot(hm, wup_ref[:, sl]), 0.0)
        x2 = x2 + _dot((u * u).astype(BF16), wdn_ref[sl, :])
    gate = _sigmoid(_dot(_rms(x2, pg_ref[...]).astype(BF16), wpg_ref[...]))
    x3 = x2 + gate * _dot(p_ref[...].astype(BF16), wple_ref[...])
    out_ref[...] = _rms(x3, fg_ref[...])


def _post(x, o, ry, sga, sgr, p, wao, wro, wout, mg, wup, wdn, pg, wpg, wple, fg, tm):
    s, d = x.shape
    row = lambda a: pl.BlockSpec((tm, a.shape[1]), lambda i: (i, 0))
    const = lambda a: _const_spec(a.shape)
    return pl.pallas_call(
        _post_kernel, grid=(s // tm,),
        in_specs=[row(x), row(o), row(ry), row(sga), row(sgr), row(p), const(wao), const(wro), const(wout),
                  const(mg), const(wup), const(wdn), const(pg), const(wpg), const(wple), const(fg)],
        out_specs=pl.BlockSpec((tm, d), lambda i: (i, 0)),
        out_shape=jax.ShapeDtypeStruct((s, d), F32),
        compiler_params=pltpu.CompilerParams(dimension_semantics=("arbitrary",), vmem_limit_bytes=VMEM_LIMIT),
        name="post",
    )(x, o, ry, sga, sgr, p, wao, wro, wout, mg, wup, wdn, pg, wpg, wple, fg)


def _rope_tables(seq_len, head_dim):
    n_axis = head_dim // 4
    lane = jnp.arange(LANES)
    freqs = ROPE_THETA ** (-(lane % n_axis).astype(F32) / n_axis)
    sign = jnp.where((lane % head_dim) < head_dim // 2, -1.0, 1.0).astype(F32)
    rows = jnp.arange(seq_len // GRID_W, dtype=F32)[:, None] * freqs
    cols = jnp.arange(GRID_W, dtype=F32)[:, None] * freqs
    return (jnp.cos(rows), jnp.cos(cols)), (jnp.sin(rows) * sign, jnp.sin(cols) * sign)


def _layer(x, p, mix_norm, w_in, attn_q_norm, attn_k_norm, ret_decay_logit, ret_norm_gain, w_attn_o, w_ret_o,
           w_out, mlp_norm, w_up, w_down, ple_norm, w_ple_gate, w_ple, out_gain, rope_a, rope_r, tk, tq, c, tm):
    reps = LANES // ATTN_HEAD_DIM
    lane = jnp.arange(LANES) // ATTN_HEAD_DIM
    gm = (lane[:, None] == lane[None, :]).astype(BF16)
    row2 = lambda a: a.reshape(1, -1)
    qT, qn2, k, kn2, vT, qr, kr, vr, gr, sga, sgr = _inproj(
        x, row2(mix_norm), w_in.astype(BF16), row2(jnp.tile(attn_q_norm, reps)), row2(jnp.tile(attn_k_norm, reps)),
        rope_a, rope_r, gm, tk)
    o = _attention(qT, qn2, k, vT, kn2, tq)
    ry = _retention(ret_decay_logit, qr, kr, vr, gr, row2(ret_norm_gain), c, min(RET_ROWS, x.shape[0]))
    return _post(x, o, ry, sga, sgr, p, w_attn_o.astype(BF16), w_ret_o.astype(BF16), w_out.astype(BF16),
                 row2(mlp_norm), w_up.astype(BF16), w_down.astype(BF16), row2(ple_norm),
                 w_ple_gate.astype(BF16), w_ple.astype(BF16), row2(out_gain), tm)


def kernel(x, p, mix_norm, w_in, attn_q_norm, attn_k_norm, ret_decay_logit, ret_norm_gain, w_attn_o, w_ret_o,
           w_out, mlp_norm, w_up, w_down, ple_norm, w_ple_gate, w_ple, final_norm):
    b, s, d = x.shape
    depth = p.shape[0]
    assert b == 1 and depth == 1, "single sequence, single layer"
    y = _layer(x[0], p[0, 0], mix_norm[0], w_in[0], attn_q_norm[0], attn_k_norm[0], ret_decay_logit[0],
               ret_norm_gain[0], w_attn_o[0], w_ret_o[0], w_out[0], mlp_norm[0], w_up[0], w_down[0], ple_norm[0],
               w_ple_gate[0], w_ple[0], final_norm, _rope_tables(s, ATTN_HEAD_DIM), _rope_tables(s, RET_HEAD_DIM),
               min(ATTN_K_TILE, s), min(ATTN_Q_TILE, s), min(RET_CHUNK, s), min(ROW_TILE, s))
    return y[None]
```

```python
import jax
import jax.numpy as jnp
from jax import lax
from jax.experimental import pallas as pl
from jax.experimental.pallas import tpu as pltpu

F32 = jnp.float32
BF16 = jnp.bfloat16

GRID_W = 64
ATTN_HEAD_DIM = 64
ATTN_HEADS = 8
ATTN_KV_HEADS = 2
ATTN_GROUP = ATTN_HEADS // ATTN_KV_HEADS
RET_HEAD_DIM = 128
RET_HEADS = 4
ROPE_THETA = 10000.0
NORM_EPS = 1e-6
GN_EPS = 1e-5
LOG2E = 1.4426950408889634
SHIFT_LIMIT = 100.0

LANES = 128
MXU_COLS = 256
BF16_SUBLANES = 16
Q_BIAS_END = ATTN_HEAD_DIM + BF16_SUBLANES
VMEM_LIMIT = 56 * 1024 * 1024

ROW_TILE = 512
ATTN_Q_TILE = 512
ATTN_K_TILE = 512
RET_CHUNK = 256
RET_ROWS = 1024


def _dot(a, b):
    return jnp.dot(a, b, preferred_element_type=F32)


def _dot_nt(a, b):
    return lax.dot_general(a, b, (((1,), (1,)), ((), ())), preferred_element_type=F32)


def _dot_tn(a, b):
    return lax.dot_general(a, b, (((0,), (0,)), ((), ())), preferred_element_type=F32)


def _rms(x, gain):
    return x * lax.rsqrt(jnp.mean(x * x, axis=-1, keepdims=True) + NORM_EPS) * gain


def _sigmoid(x):
    return 1.0 / (1.0 + jnp.exp(-x))


def _const_spec(shape):
    return pl.BlockSpec(shape, lambda *_: (0,) * len(shape), pipeline_mode=pl.Buffered(1))


def _inproj_kernel(x_ref, g_ref, w_ref, qg_ref, kg_ref, rope_a_refs, rope_r_refs, gm_ref,
                   qT_ref, qn2_ref, k_ref, kn2_ref, vT_ref, qr_ref, kr_ref, vr_ref, gr_ref, sga_ref, sgr_ref):
    tm = x_ref.shape[0]
    h = _rms(x_ref[...], g_ref[...]).astype(BF16)
    lane = lax.broadcasted_iota(jnp.int32, (tm, LANES), 1)
    first_half = (lane % ATTN_HEAD_DIM) < (ATTN_HEAD_DIM // 2)
    gm = gm_ref[...]

    def rope_table(rows_ref, cols_ref, head_dim):
        lane64 = lax.broadcasted_iota(jnp.int32, (GRID_W, LANES), 1)
        row_kind = (lane64 % (head_dim // 2)) < (head_dim // 4)
        cols = cols_ref[...]
        return jnp.concatenate([jnp.where(row_kind, jnp.broadcast_to(rows_ref[r:r + 1, :], (GRID_W, LANES)), cols)
                                for r in range(tm // GRID_W)], axis=0)

    ca, sa = (rope_table(rows, cols, ATTN_HEAD_DIM) for rows, cols in rope_a_refs)
    cr, sr = (rope_table(rows, cols, RET_HEAD_DIM) for rows, cols in rope_r_refs)

    def proj_pair(c):
        p = _dot(h, w_ref[:, c * LANES:c * LANES + MXU_COLS])
        return p[:, :LANES], p[:, LANES:]

    def head_norm_rope(a, gain):
        sq = a * a
        hi = sq.astype(BF16)
        lo = (sq - hi.astype(F32)).astype(BF16)
        ss = _dot(hi, gm) + _dot(lo, gm)
        n = a * lax.rsqrt(ss * (1.0 / ATTN_HEAD_DIM) + NORM_EPS) * gain
        partner = jnp.where(first_half, pltpu.roll(n, LANES - ATTN_HEAD_DIM // 2, 1),
                            pltpu.roll(n, ATTN_HEAD_DIM // 2, 1))
        return n * ca + partner * sa

    def ret_rope(a):
        return a * cr + pltpu.roll(a, RET_HEAD_DIM // 2, 1) * sr

    zeros = jnp.zeros((ATTN_HEAD_DIM, tm), BF16)

    def attn_q(c0, slabs):
        for c, a in zip((c0, c0 + 1), slabs):
            q = head_norm_rope(a, qg_ref[...]) * (ATTN_HEAD_DIM ** -0.5 * LOG2E)
            qt = q.T.astype(BF16)
            for j in range(2):
                head = 2 * c + j
                rows = qt[j * ATTN_HEAD_DIM:(j + 1) * ATTN_HEAD_DIM]
                qT_ref[head, 0:ATTN_HEAD_DIM, :] = rows
                qT_ref[head, ATTN_HEAD_DIM:LANES, :] = zeros
                rf = rows.astype(F32)
                qn2_ref[head] = jnp.sum(rf * rf, axis=0, keepdims=True)

    def attn_kv(_, slabs):
        k_slab, v_slab = slabs
        kf = head_norm_rope(k_slab, kg_ref[...]).astype(BF16).astype(F32)
        ksq = kf * kf
        khi = ksq.astype(BF16)
        kss = _dot(khi, gm) + _dot((ksq - khi.astype(F32)).astype(BF16), gm)
        k_sw, kss_sw = pltpu.roll(kf, ATTN_HEAD_DIM, 1), pltpu.roll(kss, ATTN_HEAD_DIM, 1)
        low = lane < ATTN_HEAD_DIM
        one_lane = jnp.where(lane == ATTN_HEAD_DIM, 1.0, 0.0)
        for g in range(ATTN_KV_HEADS):
            k_ref[g] = jnp.where(low, k_sw if g else kf, one_lane).astype(BF16)
            ss_g = jnp.where(low, kss_sw, kss) if g else jnp.where(low, kss, kss_sw)
            kn2_ref[g, 0] = jnp.broadcast_to(jnp.max(ss_g, axis=0, keepdims=True), (8, LANES))
        vt = v_slab.T.astype(BF16)
        for g in range(ATTN_KV_HEADS):
            vT_ref[g, 0] = vt[g * ATTN_HEAD_DIM:(g + 1) * ATTN_HEAD_DIM]

    def plain(out_ref, fn):
        def task(c0, slabs):
            for c, a in zip((c0, c0 + 1), slabs):
                out_ref[:, c * LANES:(c + 1) * LANES] = fn(a).astype(BF16)
        return task

    nd = sga_ref.shape[1] // LANES
    sections = [
        (ATTN_HEADS * ATTN_HEAD_DIM // LANES, attn_q),
        (2, attn_kv),
        (RET_HEADS, plain(qr_ref, lambda a: ret_rope(a) * (RET_HEAD_DIM ** -0.5))),
        (RET_HEADS, plain(kr_ref, ret_rope)),
        (RET_HEADS, plain(vr_ref, lambda a: a)),
        (RET_HEADS, plain(gr_ref, lambda a: a * _sigmoid(a))),
        (nd, plain(sga_ref, _sigmoid)),
        (nd, plain(sgr_ref, _sigmoid)),
    ]
    tasks, col = [], 0
    for count, consumer in sections:
        tasks += [(col + c0, c0, consumer) for c0 in range(0, count, 2)]
        col += count
    pending = proj_pair(tasks[0][0])
    for i, (_, c0, consumer) in enumerate(tasks):
        slabs = pending
        if i + 1 < len(tasks):
            pending = proj_pair(tasks[i + 1][0])
        consumer(c0, slabs)


def _inproj(x, gain, w, qg, kg, rope_a, rope_r, gm, tm):
    s, d = x.shape
    n_in = w.shape[1]
    rw = RET_HEADS * RET_HEAD_DIM
    assert tm % GRID_W == 0
    row = lambda width: pl.BlockSpec((tm, width), lambda i: (i, 0))
    rope_spec = ((pl.BlockSpec((tm // GRID_W, LANES), lambda i: (i, 0)), _const_spec((GRID_W, LANES))),) * 2
    out_shape = (
        jax.ShapeDtypeStruct((ATTN_HEADS, LANES, s), BF16),
        jax.ShapeDtypeStruct((ATTN_HEADS, 1, s), F32),
        jax.ShapeDtypeStruct((ATTN_KV_HEADS, s, LANES), BF16),
        jax.ShapeDtypeStruct((ATTN_KV_HEADS, s // tm, 8, LANES), F32),
        jax.ShapeDtypeStruct((ATTN_KV_HEADS, s // tm, ATTN_HEAD_DIM, tm), BF16),
        jax.ShapeDtypeStruct((s, rw), BF16),
        jax.ShapeDtypeStruct((s, rw), BF16),
        jax.ShapeDtypeStruct((s, rw), BF16),
        jax.ShapeDtypeStruct((s, rw), BF16),
        jax.ShapeDtypeStruct((s, d), BF16),
        jax.ShapeDtypeStruct((s, d), BF16),
    )
    out_specs = (
        pl.BlockSpec((ATTN_HEADS, LANES, tm), lambda i: (0, 0, i)),
        pl.BlockSpec((ATTN_HEADS, 1, tm), lambda i: (0, 0, i)),
        pl.BlockSpec((ATTN_KV_HEADS, tm, LANES), lambda i: (0, i, 0)),
        pl.BlockSpec((ATTN_KV_HEADS, 1, 8, LANES), lambda i: (0, i, 0, 0)),
        pl.BlockSpec((ATTN_KV_HEADS, 1, ATTN_HEAD_DIM, tm), lambda i: (0, i, 0, 0)),
        row(rw), row(rw), row(rw), row(rw), row(d), row(d),
    )
    in_specs = [
        row(d), _const_spec((1, d)), _const_spec((d, n_in)),
        _const_spec((1, LANES)), _const_spec((1, LANES)),
        rope_spec, rope_spec,
        _const_spec((LANES, LANES)),
    ]
    return pl.pallas_call(
        _inproj_kernel, grid=(s // tm,), in_specs=in_specs, out_specs=out_specs, out_shape=out_shape,
        compiler_params=pltpu.CompilerParams(dimension_semantics=("arbitrary",), vmem_limit_bytes=VMEM_LIMIT),
        name="inproj",
    )(x, gain, w, qg, kg, rope_a, rope_r, gm)


def _attn_kernel(qT_ref, qn2_ref, k_ref, vT_ref, kn2_ref, o_ref, qa_scr, acc_scr, l_scr, p_scr, m_scr, s_scr):
    nk, tk = vT_ref.shape[1], vT_ref.shape[3]
    tq = qT_ref.shape[2]
    acc_scr[...] = jnp.zeros(acc_scr.shape, F32)
    l_scr[...] = jnp.zeros(l_scr.shape, F32)

    def key_block(j):
        return k_ref[0, pl.ds(pl.multiple_of(j * tk, tk), tk), :]

    def sublane_sums(p):
        return jnp.sum(p.reshape(p.shape[0] // 8, 8, tq), axis=0)

    kmax2 = jnp.max(jnp.max(kn2_ref[0], axis=0), axis=0, keepdims=True)
    kmax2 = jnp.concatenate([kmax2] * (tq // LANES), axis=1)
    row = lax.broadcasted_iota(jnp.int32, (BF16_SUBLANES, tq), 0)
    bmax = jnp.zeros((1, tq), F32)
    for h in range(ATTN_GROUP):
        b = jnp.sqrt(qn2_ref[h] * kmax2)
        bmax = jnp.maximum(bmax, b)
        qa_scr[h, 0:ATTN_HEAD_DIM, :] = qT_ref[h, 0:ATTN_HEAD_DIM, :]
        qa_scr[h, ATTN_HEAD_DIM:Q_BIAS_END, :] = jnp.where(row == 0, -b, 0.0).astype(BF16)
        qa_scr[h, Q_BIAS_END:LANES, :] = jnp.zeros((LANES - Q_BIAS_END, tq), BF16)
    shift_ok = 2.0 * jnp.max(bmax) <= SHIFT_LIMIT

    @pl.when(shift_ok)
    def _shifted():
        half = tk // 2
        lo, hi = slice(0, half), slice(half, tk)

        def step(j):
            kb, vb = key_block(j), vT_ref[0, j]
            vb_prev = vT_ref[0, jnp.maximum(j - 1, 0)]
            for rows, pv_rows, pv_v in ((lo, hi, vb_prev), (hi, lo, vb)):
                for h in range(ATTN_GROUP):
                    p = jnp.exp2(_dot(kb[rows], qa_scr[h]))
                    l_scr[h] += sublane_sums(p)
                    acc_scr[h] += _dot(pv_v[:, pv_rows], p_scr[h, pv_rows, :])
                    p_scr[h, rows, :] = p.astype(BF16)

        p_scr[:, hi, :] = jnp.zeros((ATTN_GROUP, half, tq), BF16)

        unroll = max(u for u in (1, 2, 4, 8, 16) if nk % u == 0)

        def body(i, carry):
            for u in range(unroll):
                step(unroll * i + u)
            return carry

        lax.fori_loop(0, nk // unroll, body, 0)
        vb = vT_ref[0, nk - 1]
        for h in range(ATTN_GROUP):
            acc_scr[h] += _dot(vb[:, hi], p_scr[h, hi, :])

    @pl.when(jnp.logical_not(shift_ok))
    def _online():
        m_scr[...] = jnp.full(m_scr.shape, -jnp.inf, F32)

        def scores(j, buf):
            kb = key_block(j)
            for h in range(ATTN_GROUP):
                s_scr[buf, h] = _dot(kb, qT_ref[h])

        def update(j, buf):
            vb = vT_ref[0, j]
            ps, alphas = [], []
            for h in range(ATTN_GROUP):
                s = s_scr[buf, h]
                m_prev = m_scr[h]
                m_new = jnp.maximum(m_prev, jnp.max(s, axis=0, keepdims=True))
                p = jnp.exp2(s - m_new)
                alpha = jnp.exp2(m_prev - m_new)
                l_scr[h] = l_scr[h] * alpha + sublane_sums(p)
                ps.append(p.astype(BF16))
                alphas.append(alpha)
                m_scr[h] = m_new
            for h in range(ATTN_GROUP):
                acc_scr[h] = acc_scr[h] * alphas[h] + _dot(vb, ps[h])

        scores(0, 0)

        def body(i, carry):
            j = 2 * i
            scores(j + 1, 1)
            update(j, 0)
            scores(jnp.minimum(j + 2, nk - 1), 0)
            update(j + 1, 1)
            return carry

        lax.fori_loop(0, nk // 2, body, 0)

    outs = [acc_scr[h] * (1.0 / jnp.sum(l_scr[h], axis=0, keepdims=True)) for h in range(ATTN_GROUP)]
    o_ref[...] = jnp.concatenate(outs, axis=0).T.astype(o_ref.dtype)


def _attention(qT, qn2, k, vT, kn2, tq):
    s = k.shape[1]
    nk, tk = vT.shape[1], vT.shape[3]
    assert nk % 2 == 0 and tq % LANES == 0
    gw = ATTN_GROUP * ATTN_HEAD_DIM
    return pl.pallas_call(
        _attn_kernel,
        grid=(ATTN_KV_HEADS, s // tq),
        in_specs=[
            pl.BlockSpec((ATTN_GROUP, LANES, tq), lambda g, i: (g, 0, i)),
            pl.BlockSpec((ATTN_GROUP, 1, tq), lambda g, i: (g, 0, i)),
            pl.BlockSpec((1, s, LANES), lambda g, i: (g, 0, 0)),
            pl.BlockSpec((1, nk, ATTN_HEAD_DIM, tk), lambda g, i: (g, 0, 0, 0)),
            pl.BlockSpec((1, nk, 8, LANES), lambda g, i: (g, 0, 0, 0)),
        ],
        out_specs=pl.BlockSpec((tq, gw), lambda g, i: (i, g)),
        out_shape=jax.ShapeDtypeStruct((s, ATTN_HEADS * ATTN_HEAD_DIM), BF16),
        scratch_shapes=[pltpu.VMEM((ATTN_GROUP, LANES, tq), BF16), pltpu.VMEM((ATTN_GROUP, ATTN_HEAD_DIM, tq), F32),
                        pltpu.VMEM((ATTN_GROUP, 8, tq), F32),
                        pltpu.VMEM((ATTN_GROUP, tk, tq), BF16), pltpu.VMEM((ATTN_GROUP, 1, tq), F32),
                        pltpu.VMEM((2, ATTN_GROUP, tk, tq), F32)],
        compiler_params=pltpu.CompilerParams(dimension_semantics=("arbitrary", "arbitrary"),
                                             vmem_limit_bytes=VMEM_LIMIT),
        name="attention",
    )(qT, qn2, k, vT, kn2)


def _retention_kernel(dl_ref, q_ref, k_ref, v_ref, g_ref, gain_ref, o_ref,
                      sf_scr, sb_scr, sball_scr, dmask_scr, kdf_scr, kdb_scr, qdf_scr, qdb_scr, cd_scr):
    ph, n = pl.program_id(0), pl.program_id(1)
    nb = pl.num_programs(1)
    c = dmask_scr.shape[1]
    nsub = q_ref.shape[0] // c
    hd = RET_HEAD_DIM

    @pl.when((ph == 0) & (n == 0))
    def _init():
        sf_scr[...] = jnp.zeros(sf_scr.shape, F32)
        sb_scr[...] = jnp.zeros(sb_scr.shape, F32)
        pos = lax.broadcasted_iota(jnp.int32, (c, hd), 0).astype(F32)
        ii = lax.broadcasted_iota(jnp.int32, (c, c), 0)
        jj = lax.broadcasted_iota(jnp.int32, (c, c), 1)
        diff = (ii - jj).astype(F32)

        def log_sigmoid(shape, x):
            v = jnp.full(shape, x, F32)
            return -(jnp.maximum(-v, 0.0) + jnp.log(1.0 + jnp.exp(-jnp.abs(v))))

        for h in range(RET_HEADS):
            lgf, lgb = log_sigmoid((c, hd), dl_ref[0, h]), log_sigmoid((c, hd), dl_ref[1, h])
            kdf_scr[h] = jnp.exp(lgf * (c - 1.0 - pos))
            qdf_scr[h] = jnp.exp(lgf * (pos + 1.0))
            kdb_scr[h] = jnp.exp(lgb * pos)
            qdb_scr[h] = jnp.exp(lgb * (c - pos))
            cd_scr[0, h] = jnp.exp(lgf[0:8] * float(c))
            cd_scr[1, h] = jnp.exp(lgb[0:8] * float(c))
            lgf2, lgb2 = log_sigmoid((c, c), dl_ref[0, h]), log_sigmoid((c, c), dl_ref[1, h])
            dmask_scr[h] = jnp.where(diff >= 0, jnp.exp(lgf2 * jnp.maximum(diff, 0.0)),
                                     jnp.exp(lgb2 * jnp.maximum(-diff, 0.0)))

    @pl.when(ph == 0)
    def _backward_states():
        for sub in reversed(range(nsub)):
            rows = slice(sub * c, (sub + 1) * c)
            m = (nb - 1 - n) * nsub + sub
            sball_scr[m] = sb_scr[...].astype(BF16)
            for h in range(RET_HEADS):
                sl = slice(h * hd, (h + 1) * hd)
                kd = (k_ref[rows, sl].astype(F32) * kdb_scr[h]).astype(BF16)
                sb_scr[h] = sb_scr[h] * cd_scr[1, h, 0:1, :] + _dot_tn(kd, v_ref[rows, sl])

    @pl.when(ph == 1)
    def _forward():
        for sub in range(nsub):
            rows = slice(sub * c, (sub + 1) * c)
            m = n * nsub + sub
            for h in range(RET_HEADS):
                sl = slice(h * hd, (h + 1) * hd)
                q, k, v = q_ref[rows, sl], k_ref[rows, sl], v_ref[rows, sl]
                qf, kf = q.astype(F32), k.astype(F32)
                a = (_dot_nt(q, k) * dmask_scr[h]).astype(BF16)
                y = _dot(a, v)
                y += _dot((qf * qdf_scr[h]).astype(BF16), sf_scr[h].astype(BF16))
                y += _dot((qf * qdb_scr[h]).astype(BF16), sball_scr[m, h])
                sf_scr[h] = sf_scr[h] * cd_scr[0, h, 0:1, :] + _dot_tn((kf * kdf_scr[h]).astype(BF16), v)
                mu = jnp.mean(y, axis=-1, keepdims=True)
                d = y - mu
                var = jnp.mean(d * d, axis=-1, keepdims=True)
                yn = d * lax.rsqrt(var + GN_EPS)
                o_ref[rows, sl] = (yn * gain_ref[:, sl] * g_ref[rows, sl].astype(F32)).astype(o_ref.dtype)


def _retention(decay_logit, q, k, v, g, gain, c, rows):
    s, w = q.shape
    nc, nb = s // c, s // rows
    assert rows % c == 0
    fwd_only = lambda ph, n: (ph * n, 0)
    both = lambda ph, n: (ph * n + (1 - ph) * (nb - 1 - n), 0)
    hd = RET_HEAD_DIM
    return pl.pallas_call(
        _retention_kernel,
        grid=(2, nb),
        in_specs=[
            pl.BlockSpec(memory_space=pltpu.SMEM),
            pl.BlockSpec((rows, w), fwd_only), pl.BlockSpec((rows, w), both), pl.BlockSpec((rows, w), both),
            pl.BlockSpec((rows, w), fwd_only), _const_spec((1, w)),
        ],
        out_specs=pl.BlockSpec((rows, w), fwd_only),
        out_shape=jax.ShapeDtypeStruct((s, w), BF16),
        scratch_shapes=[
            pltpu.VMEM((RET_HEADS, hd, hd), F32), pltpu.VMEM((RET_HEADS, hd, hd), F32),
            pltpu.VMEM((nc, RET_HEADS, hd, hd), BF16), pltpu.VMEM((RET_HEADS, c, c), F32),
            pltpu.VMEM((RET_HEADS, c, hd), F32), pltpu.VMEM((RET_HEADS, c, hd), F32),
            pltpu.VMEM((RET_HEADS, c, hd), F32), pltpu.VMEM((RET_HEADS, c, hd), F32),
            pltpu.VMEM((2, RET_HEADS, 8, hd), F32),
        ],
        compiler_params=pltpu.CompilerParams(dimension_semantics=("arbitrary", "arbitrary"),
                                             vmem_limit_bytes=VMEM_LIMIT),
        name="retention",
    )(decay_logit, q, k, v, g, gain)


def _post_kernel(x_ref, o_ref, ry_ref, sga_ref, sgr_ref, p_ref, wao_ref, wro_ref, wout_ref, mg_ref,
                 wup_ref, wdn_ref, pg_ref, wpg_ref, wple_ref, fg_ref, out_ref):
    d = x_ref.shape[1]
    merged = (sga_ref[...].astype(F32) * _dot(o_ref[...], wao_ref[...])
              + sgr_ref[...].astype(F32) * _dot(ry_ref[...], wro_ref[...]))
    x1 = x_ref[...] + _dot(merged.astype(BF16), wout_ref[...])
    hm = _rms(x1, mg_ref[...]).astype(BF16)
    x2 = x1
    for c in range(wup_ref.shape[1] // d):
        sl = slice(c * d, (c + 1) * d)
        u = jnp.maximum(_dot(hm, wup_ref[:, sl]), 0.0)
        x2 = x2 + _dot((u * u).astype(BF16), wdn_ref[sl, :])
    gate = _sigmoid(_dot(_rms(x2, pg_ref[...]).astype(BF16), wpg_ref[...]))
    x3 = x2 + gate * _dot(p_ref[...].astype(BF16), wple_ref[...])
    out_ref[...] = _rms(x3, fg_ref[...])


def _post(x, o, ry, sga, sgr, p, wao, wro, wout, mg, wup, wdn, pg, wpg, wple, fg, tm):
    s, d = x.shape
    row = lambda a: pl.BlockSpec((tm, a.shape[1]), lambda i: (i, 0))
    const = lambda a: _const_spec(a.shape)
    return pl.pallas_call(
        _post_kernel, grid=(s // tm,),
        in_specs=[row(x), row(o), row(ry), row(sga), row(sgr), row(p), const(wao), const(wro), const(wout),
                  const(mg), const(wup), const(wdn), const(pg), const(wpg), const(wple), const(fg)],
        out_specs=pl.BlockSpec((tm, d), lambda i: (i, 0)),
        out_shape=jax.ShapeDtypeStruct((s, d), F32),
        compiler_params=pltpu.CompilerParams(dimension_semantics=("arbitrary",), vmem_limit_bytes=VMEM_LIMIT),
        name="post",
    )(x, o, ry, sga, sgr, p, wao, wro, wout, mg, wup, wdn, pg, wpg, wple, fg)


def _rope_tables(seq_len, head_dim):
    n_axis = head_dim // 4
    lane = jnp.arange(LANES)
    freqs = ROPE_THETA ** (-(lane % n_axis).astype(F32) / n_axis)
    sign = jnp.where((lane % head_dim) < head_dim // 2, -1.0, 1.0).astype(F32)
    rows = jnp.arange(seq_len // GRID_W, dtype=F32)[:, None] * freqs
    cols = jnp.arange(GRID_W, dtype=F32)[:, None] * freqs
    return (jnp.cos(rows), jnp.cos(cols)), (jnp.sin(rows) * sign, jnp.sin(cols) * sign)


def _layer(x, p, mix_norm, w_in, attn_q_norm, attn_k_norm, ret_decay_logit, ret_norm_gain, w_attn_o, w_ret_o,
           w_out, mlp_norm, w_up, w_down, ple_norm, w_ple_gate, w_ple, out_gain, rope_a, rope_r, tk, tq, c, tm):
    reps = LANES // ATTN_HEAD_DIM
    lane = jnp.arange(LANES) // ATTN_HEAD_DIM
    gm = (lane[:, None] == lane[None, :]).astype(BF16)
    row2 = lambda a: a.reshape(1, -1)
    qT, qn2, k, kn2, vT, qr, kr, vr, gr, sga, sgr = _inproj(
        x, row2(mix_norm), w_in.astype(BF16), row2(jnp.tile(attn_q_norm, reps)), row2(jnp.tile(attn_k_norm, reps)),
        rope_a, rope_r, gm, tk)
    o = _attention(qT, qn2, k, vT, kn2, tq)
    ry = _retention(ret_decay_logit, qr, kr, vr, gr, row2(ret_norm_gain), c, min(RET_ROWS, x.shape[0]))
    return _post(x, o, ry, sga, sgr, p, w_attn_o.astype(BF16), w_ret_o.astype(BF16), w_out.astype(BF16),
                 row2(mlp_norm), w_up.astype(BF16), w_down.astype(BF16), row2(ple_norm),
                 w_ple_gate.astype(BF16), w_ple.astype(BF16), row2(out_gain), tm)


def kernel(x, p, mix_norm, w_in, attn_q_norm, attn_k_norm, ret_decay_logit, ret_norm_gain, w_attn_o, w_ret_o,
           w_out, mlp_norm, w_up, w_down, ple_norm, w_ple_gate, w_ple, final_norm):
    b, s, d = x.shape
    depth = p.shape[0]
    assert b == 1 and depth == 1, "single sequence, single layer"
    y = _layer(x[0], p[0, 0], mix_norm[0], w_in[0], attn_q_norm[0], attn_k_norm[0], ret_decay_logit[0],
               ret_norm_gain[0], w_attn_o[0], w_ret_o[0], w_out[0], mlp_norm[0], w_up[0], w_down[0], ple_norm[0],
               w_ple_gate[0], w_ple[0], final_norm, _rope_tables(s, ATTN_HEAD_DIM), _rope_tables(s, RET_HEAD_DIM),
               min(ATTN_K_TILE, s), min(ATTN_Q_TILE, s), min(RET_CHUNK, s), min(ROW_TILE, s))
    return y[None]
```

```python
import jax
import jax.numpy as jnp
from jax import lax
from jax.experimental import pallas as pl
from jax.experimental.pallas import tpu as pltpu

F32 = jnp.float32
BF16 = jnp.bfloat16

GRID_W = 64
ATTN_HEAD_DIM = 64
ATTN_HEADS = 8
ATTN_KV_HEADS = 2
ATTN_GROUP = ATTN_HEADS // ATTN_KV_HEADS
RET_HEAD_DIM = 128
RET_HEADS = 4
ROPE_THETA = 10000.0
NORM_EPS = 1e-6
GN_EPS = 1e-5
LOG2E = 1.4426950408889634
SHIFT_LIMIT = 100.0

LANES = 128
MXU_COLS = 256
BF16_SUBLANES = 16
Q_BIAS_END = ATTN_HEAD_DIM + BF16_SUBLANES
VMEM_LIMIT = 56 * 1024 * 1024

ROW_TILE = 512
ATTN_Q_TILE = 512
ATTN_K_TILE = 512
RET_CHUNK = 256
RET_ROWS = 2048


def _dot(a, b):
    return jnp.dot(a, b, preferred_element_type=F32)


def _dot_nt(a, b):
    return lax.dot_general(a, b, (((1,), (1,)), ((), ())), preferred_element_type=F32)


def _dot_tn(a, b):
    return lax.dot_general(a, b, (((0,), (0,)), ((), ())), preferred_element_type=F32)


def _rms(x, gain):
    return x * lax.rsqrt(jnp.mean(x * x, axis=-1, keepdims=True) + NORM_EPS) * gain


def _sigmoid(x):
    return 1.0 / (1.0 + jnp.exp(-x))


def _const_spec(shape):
    return pl.BlockSpec(shape, lambda *_: (0,) * len(shape), pipeline_mode=pl.Buffered(1))


def _inproj_kernel(x_ref, g_ref, w_ref, qg_ref, kg_ref, rope_a_refs, rope_r_refs, gm_ref,
                   qT_ref, qn2_ref, k_ref, kn2_ref, vT_ref, qr_ref, kr_ref, vr_ref, gr_ref, sga_ref, sgr_ref):
    tm = x_ref.shape[0]
    h = _rms(x_ref[...], g_ref[...]).astype(BF16)
    lane = lax.broadcasted_iota(jnp.int32, (tm, LANES), 1)
    first_half = (lane % ATTN_HEAD_DIM) < (ATTN_HEAD_DIM // 2)
    gm = gm_ref[...]

    def rope_table(rows_ref, cols_ref, head_dim):
        lane64 = lax.broadcasted_iota(jnp.int32, (GRID_W, LANES), 1)
        row_kind = (lane64 % (head_dim // 2)) < (head_dim // 4)
        cols = cols_ref[...]
        return jnp.concatenate([jnp.where(row_kind, jnp.broadcast_to(rows_ref[r:r + 1, :], (GRID_W, LANES)), cols)
                                for r in range(tm // GRID_W)], axis=0)

    ca, sa = (rope_table(rows, cols, ATTN_HEAD_DIM) for rows, cols in rope_a_refs)
    cr, sr = (rope_table(rows, cols, RET_HEAD_DIM) for rows, cols in rope_r_refs)

    def proj_pair(c):
        p = _dot(h, w_ref[:, c * LANES:c * LANES + MXU_COLS])
        return p[:, :LANES], p[:, LANES:]

    def head_norm_rope(a, gain):
        sq = a * a
        hi = sq.astype(BF16)
        lo = (sq - hi.astype(F32)).astype(BF16)
        ss = _dot(hi, gm) + _dot(lo, gm)
        n = a * lax.rsqrt(ss * (1.0 / ATTN_HEAD_DIM) + NORM_EPS) * gain
        partner = jnp.where(first_half, pltpu.roll(n, LANES - ATTN_HEAD_DIM // 2, 1),
                            pltpu.roll(n, ATTN_HEAD_DIM // 2, 1))
        return n * ca + partner * sa

    def ret_rope(a):
        return a * cr + pltpu.roll(a, RET_HEAD_DIM // 2, 1) * sr

    zeros = jnp.zeros((ATTN_HEAD_DIM, tm), BF16)

    def attn_q(c0, slabs):
        for c, a in zip((c0, c0 + 1), slabs):
            q = head_norm_rope(a, qg_ref[...]) * (ATTN_HEAD_DIM ** -0.5 * LOG2E)
            qt = q.T.astype(BF16)
            for j in range(2):
                head = 2 * c + j
                rows = qt[j * ATTN_HEAD_DIM:(j + 1) * ATTN_HEAD_DIM]
                qT_ref[head, 0:ATTN_HEAD_DIM, :] = rows
                qT_ref[head, ATTN_HEAD_DIM:LANES, :] = zeros
                rf = rows.astype(F32)
                qn2_ref[head] = jnp.sum(rf * rf, axis=0, keepdims=True)

    def attn_kv(_, slabs):
        k_slab, v_slab = slabs
        kf = head_norm_rope(k_slab, kg_ref[...]).astype(BF16).astype(F32)
        ksq = kf * kf
        khi = ksq.astype(BF16)
        kss = _dot(khi, gm) + _dot((ksq - khi.astype(F32)).astype(BF16), gm)
        k_sw, kss_sw = pltpu.roll(kf, ATTN_HEAD_DIM, 1), pltpu.roll(kss, ATTN_HEAD_DIM, 1)
        low = lane < ATTN_HEAD_DIM
        one_lane = jnp.where(lane == ATTN_HEAD_DIM, 1.0, 0.0)
        for g in range(ATTN_KV_HEADS):
            k_ref[g] = jnp.where(low, k_sw if g else kf, one_lane).astype(BF16)
            ss_g = jnp.where(low, kss_sw, kss) if g else jnp.where(low, kss, kss_sw)
            kn2_ref[g, 0] = jnp.broadcast_to(jnp.max(ss_g, axis=0, keepdims=True), (8, LANES))
        vt = v_slab.T.astype(BF16)
        for g in range(ATTN_KV_HEADS):
            vT_ref[g, 0] = vt[g * ATTN_HEAD_DIM:(g + 1) * ATTN_HEAD_DIM]

    def plain(out_ref, fn):
        def task(c0, slabs):
            for c, a in zip((c0, c0 + 1), slabs):
                out_ref[:, c * LANES:(c + 1) * LANES] = fn(a).astype(BF16)
        return task

    nd = sga_ref.shape[1] // LANES
    sections = [
        (ATTN_HEADS * ATTN_HEAD_DIM // LANES, attn_q),
        (2, attn_kv),
        (RET_HEADS, plain(qr_ref, lambda a: ret_rope(a) * (RET_HEAD_DIM ** -0.5))),
        (RET_HEADS, plain(kr_ref, ret_rope)),
        (RET_HEADS, plain(vr_ref, lambda a: a)),
        (RET_HEADS, plain(gr_ref, lambda a: a * _sigmoid(a))),
        (nd, plain(sga_ref, _sigmoid)),
        (nd, plain(sgr_ref, _sigmoid)),
    ]
    tasks, col = [], 0
    for count, consumer in sections:
        tasks += [(col + c0, c0, consumer) for c0 in range(0, count, 2)]
        col += count
    pending = proj_pair(tasks[0][0])
    for i, (_, c0, consumer) in enumerate(tasks):
        slabs = pending
        if i + 1 < len(tasks):
            pending = proj_pair(tasks[i + 1][0])
        consumer(c0, slabs)


def _inproj(x, gain, w, qg, kg, rope_a, rope_r, gm, tm):
    s, d = x.shape
    n_in = w.shape[1]
    rw = RET_HEADS * RET_HEAD_DIM
    assert tm % GRID_W == 0
    row = lambda width: pl.BlockSpec((tm, width), lambda i: (i, 0))
    rope_spec = ((pl.BlockSpec((tm // GRID_W, LANES), lambda i: (i, 0)), _const_spec((GRID_W, LANES))),) * 2
    out_shape = (
        jax.ShapeDtypeStruct((ATTN_HEADS, LANES, s), BF16),
        jax.ShapeDtypeStruct((ATTN_HEADS, 1, s), F32),
        jax.ShapeDtypeStruct((ATTN_KV_HEADS, s, LANES), BF16),
        jax.ShapeDtypeStruct((ATTN_KV_HEADS, s // tm, 8, LANES), F32),
        jax.ShapeDtypeStruct((ATTN_KV_HEADS, s // tm, ATTN_HEAD_DIM, tm), BF16),
        jax.ShapeDtypeStruct((s, rw), BF16),
        jax.ShapeDtypeStruct((s, rw), BF16),
        jax.ShapeDtypeStruct((s, rw), BF16),
        jax.ShapeDtypeStruct((s, rw), BF16),
        jax.ShapeDtypeStruct((s, d), BF16),
        jax.ShapeDtypeStruct((s, d), BF16),
    )
    out_specs = (
        pl.BlockSpec((ATTN_HEADS, LANES, tm), lambda i: (0, 0, i)),
        pl.BlockSpec((ATTN_HEADS, 1, tm), lambda i: (0, 0, i)),
        pl.BlockSpec((ATTN_KV_HEADS, tm, LANES), lambda i: (0, i, 0)),
        pl.BlockSpec((ATTN_KV_HEADS, 1, 8, LANES), lambda i: (0, i, 0, 0)),
        pl.BlockSpec((ATTN_KV_HEADS, 1, ATTN_HEAD_DIM, tm), lambda i: (0, i, 0, 0)),
        row(rw), row(rw), row(rw), row(rw), row(d), row(d),
    )
    in_specs = [
        row(d), _const_spec((1, d)), _const_spec((d, n_in)),
        _const_spec((1, LANES)), _const_spec((1, LANES)),
        rope_spec, rope_spec,
        _const_spec((LANES, LANES)),
    ]
    return pl.pallas_call(
        _inproj_kernel, grid=(s // tm,), in_specs=in_specs, out_specs=out_specs, out_shape=out_shape,
        compiler_params=pltpu.CompilerParams(dimension_semantics=("arbitrary",), vmem_limit_bytes=VMEM_LIMIT),
        name="inproj",
    )(x, gain, w, qg, kg, rope_a, rope_r, gm)


def _attn_kernel(qT_ref, qn2_ref, k_ref, vT_ref, kn2_ref, o_ref, qa_scr, acc_scr, l_scr, p_scr, m_scr, s_scr):
    nk, tk = vT_ref.shape[1], vT_ref.shape[3]
    tq = qT_ref.shape[2]
    acc_scr[...] = jnp.zeros(acc_scr.shape, F32)
    l_scr[...] = jnp.zeros(l_scr.shape, F32)

    def key_block(j):
        return k_ref[0, pl.ds(pl.multiple_of(j * tk, tk), tk), :]

    def sublane_sums(p):
        return jnp.sum(p.reshape(p.shape[0] // 8, 8, tq), axis=0)

    kmax2 = jnp.max(jnp.max(kn2_ref[0], axis=0), axis=0, keepdims=True)
    kmax2 = jnp.concatenate([kmax2] * (tq // LANES), axis=1)
    row = lax.broadcasted_iota(jnp.int32, (BF16_SUBLANES, tq), 0)
    bmax = jnp.zeros((1, tq), F32)
    for h in range(ATTN_GROUP):
        b = jnp.sqrt(qn2_ref[h] * kmax2)
        bmax = jnp.maximum(bmax, b)
        qa_scr[h, 0:ATTN_HEAD_DIM, :] = qT_ref[h, 0:ATTN_HEAD_DIM, :]
        qa_scr[h, ATTN_HEAD_DIM:Q_BIAS_END, :] = jnp.where(row == 0, -b, 0.0).astype(BF16)
        qa_scr[h, Q_BIAS_END:LANES, :] = jnp.zeros((LANES - Q_BIAS_END, tq), BF16)
    shift_ok = 2.0 * jnp.max(bmax) <= SHIFT_LIMIT

    @pl.when(shift_ok)
    def _shifted():
        half = tk // 2
        lo, hi = slice(0, half), slice(half, tk)

        def step(j):
            kb, vb = key_block(j), vT_ref[0, j]
            vb_prev = vT_ref[0, jnp.maximum(j - 1, 0)]
            for rows, pv_rows, pv_v in ((lo, hi, vb_prev), (hi, lo, vb)):
                for h in range(ATTN_GROUP):
                    p = jnp.exp2(_dot(kb[rows], qa_scr[h]))
                    l_scr[h] += sublane_sums(p)
                    acc_scr[h] += _dot(pv_v[:, pv_rows], p_scr[h, pv_rows, :])
                    p_scr[h, rows, :] = p.astype(BF16)

        p_scr[:, hi, :] = jnp.zeros((ATTN_GROUP, half, tq), BF16)

        unroll = max(u for u in (1, 2, 4, 8, 16) if nk % u == 0)

        def body(i, carry):
            for u in range(unroll):
                step(unroll * i + u)
            return carry

        lax.fori_loop(0, nk // unroll, body, 0)
        vb = vT_ref[0, nk - 1]
        for h in range(ATTN_GROUP):
            acc_scr[h] += _dot(vb[:, hi], p_scr[h, hi, :])

    @pl.when(jnp.logical_not(shift_ok))
    def _online():
        m_scr[...] = jnp.full(m_scr.shape, -jnp.inf, F32)

        def scores(j, buf):
            kb = key_block(j)
            for h in range(ATTN_GROUP):
                s_scr[buf, h] = _dot(kb, qT_ref[h])

        def update(j, buf):
            vb = vT_ref[0, j]
            ps, alphas = [], []
            for h in range(ATTN_GROUP):
                s = s_scr[buf, h]
                m_prev = m_scr[h]
                m_new = jnp.maximum(m_prev, jnp.max(s, axis=0, keepdims=True))
                p = jnp.exp2(s - m_new)
                alpha = jnp.exp2(m_prev - m_new)
                l_scr[h] = l_scr[h] * alpha + sublane_sums(p)
                ps.append(p.astype(BF16))
                alphas.append(alpha)
                m_scr[h] = m_new
            for h in range(ATTN_GROUP):
                acc_scr[h] = acc_scr[h] * alphas[h] + _dot(vb, ps[h])

        scores(0, 0)

        def body(i, carry):
            j = 2 * i
            scores(j + 1, 1)
            update(j, 0)
            scores(jnp.minimum(j + 2, nk - 1), 0)
            update(j + 1, 1)
            return carry

        lax.fori_loop(0, nk // 2, body, 0)

    outs = [acc_scr[h] * (1.0 / jnp.sum(l_scr[h], axis=0, keepdims=True)) for h in range(ATTN_GROUP)]
    o_ref[...] = jnp.concatenate(outs, axis=0).T.astype(o_ref.dtype)


def _attention(qT, qn2, k, vT, kn2, tq):
    s = k.shape[1]
    nk, tk = vT.shape[1], vT.shape[3]
    assert nk % 2 == 0 and tq % LANES == 0
    gw = ATTN_GROUP * ATTN_HEAD_DIM
    return pl.pallas_call(
        _attn_kernel,
        grid=(ATTN_KV_HEADS, s // tq),
        in_specs=[
            pl.BlockSpec((ATTN_GROUP, LANES, tq), lambda g, i: (g, 0, i)),
            pl.BlockSpec((ATTN_GROUP, 1, tq), lambda g, i: (g, 0, i)),
            pl.BlockSpec((1, s, LANES), lambda g, i: (g, 0, 0)),
            pl.BlockSpec((1, nk, ATTN_HEAD_DIM, tk), lambda g, i: (g, 0, 0, 0)),
            pl.BlockSpec((1, nk, 8, LANES), lambda g, i: (g, 0, 0, 0)),
        ],
        out_specs=pl.BlockSpec((tq, gw), lambda g, i: (i, g)),
        out_shape=jax.ShapeDtypeStruct((s, ATTN_HEADS * ATTN_HEAD_DIM), BF16),
        scratch_shapes=[pltpu.VMEM((ATTN_GROUP, LANES, tq), BF16), pltpu.VMEM((ATTN_GROUP, ATTN_HEAD_DIM, tq), F32),
                        pltpu.VMEM((ATTN_GROUP, 8, tq), F32),
                        pltpu.VMEM((ATTN_GROUP, tk, tq), BF16), pltpu.VMEM((ATTN_GROUP, 1, tq), F32),
                        pltpu.VMEM((2, ATTN_GROUP, tk, tq), F32)],
        compiler_params=pltpu.CompilerParams(dimension_semantics=("arbitrary", "arbitrary"),
                                             vmem_limit_bytes=VMEM_LIMIT),
        name="attention",
    )(qT, qn2, k, vT, kn2)


def _retention_kernel(dl_ref, q_ref, k_ref, v_ref, g_ref, gain_ref, o_ref,
                      sf_scr, sb_scr, sball_scr, dmask_scr, kdf_scr, kdb_scr, qdf_scr, qdb_scr, cd_scr):
    ph, n = pl.program_id(0), pl.program_id(1)
    nb = pl.num_programs(1)
    c = dmask_scr.shape[1]
    nsub = q_ref.shape[0] // c
    hd = RET_HEAD_DIM

    @pl.when((ph == 0) & (n == 0))
    def _init():
        sf_scr[...] = jnp.zeros(sf_scr.shape, F32)
        sb_scr[...] = jnp.zeros(sb_scr.shape, F32)
        pos = lax.broadcasted_iota(jnp.int32, (c, hd), 0).astype(F32)
        ii = lax.broadcasted_iota(jnp.int32, (c, c), 0)
        jj = lax.broadcasted_iota(jnp.int32, (c, c), 1)
        diff = (ii - jj).astype(F32)

        def log_sigmoid(shape, x):
            v = jnp.full(shape, x, F32)
            return -(jnp.maximum(-v, 0.0) + jnp.log(1.0 + jnp.exp(-jnp.abs(v))))

        for h in range(RET_HEADS):
            lgf, lgb = log_sigmoid((c, hd), dl_ref[0, h]), log_sigmoid((c, hd), dl_ref[1, h])
            kdf_scr[h] = jnp.exp(lgf * (c - 1.0 - pos))
            qdf_scr[h] = jnp.exp(lgf * (pos + 1.0))
            kdb_scr[h] = jnp.exp(lgb * pos)
            qdb_scr[h] = jnp.exp(lgb * (c - pos))
            cd_scr[0, h] = jnp.exp(lgf[0:8] * float(c))
            cd_scr[1, h] = jnp.exp(lgb[0:8] * float(c))
            lgf2, lgb2 = log_sigmoid((c, c), dl_ref[0, h]), log_sigmoid((c, c), dl_ref[1, h])
            dmask_scr[h] = jnp.where(diff >= 0, jnp.exp(lgf2 * jnp.maximum(diff, 0.0)),
                                     jnp.exp(lgb2 * jnp.maximum(-diff, 0.0)))

    @pl.when(ph == 0)
    def _backward_states():
        for sub in reversed(range(nsub)):
            rows = slice(sub * c, (sub + 1) * c)
            m = (nb - 1 - n) * nsub + sub
            sball_scr[m] = sb_scr[...].astype(BF16)
            for h in range(RET_HEADS):
                sl = slice(h * hd, (h + 1) * hd)
                kd = (k_ref[rows, sl].astype(F32) * kdb_scr[h]).astype(BF16)
                sb_scr[h] = sb_scr[h] * cd_scr[1, h, 0:1, :] + _dot_tn(kd, v_ref[rows, sl])

    @pl.when(ph == 1)
    def _forward():
        for sub in range(nsub):
            rows = slice(sub * c, (sub + 1) * c)
            m = n * nsub + sub
            for h in range(RET_HEADS):
                sl = slice(h * hd, (h + 1) * hd)
                q, k, v = q_ref[rows, sl], k_ref[rows, sl], v_ref[rows, sl]
                qf, kf = q.astype(F32), k.astype(F32)
                a = (_dot_nt(q, k) * dmask_scr[h]).astype(BF16)
                y = _dot(a, v)
                y += _dot((qf * qdf_scr[h]).astype(BF16), sf_scr[h].astype(BF16))
                y += _dot((qf * qdb_scr[h]).astype(BF16), sball_scr[m, h])
                sf_scr[h] = sf_scr[h] * cd_scr[0, h, 0:1, :] + _dot_tn((kf * kdf_scr[h]).astype(BF16), v)
                mu = jnp.mean(y, axis=-1, keepdims=True)
                d = y - mu
                var = jnp.mean(d * d, axis=-1, keepdims=True)
                yn = d * lax.rsqrt(var + GN_EPS)
                o_ref[rows, sl] = (yn * gain_ref[:, sl] * g_ref[rows, sl].astype(F32)).astype(o_ref.dtype)


def _retention(decay_logit, q, k, v, g, gain, c, rows):
    s, w = q.shape
    nc, nb = s // c, s // rows
    assert rows % c == 0
    fwd_only = lambda ph, n: (ph * n, 0)
    both = lambda ph, n: (ph * n + (1 - ph) * (nb - 1 - n), 0)
    hd = RET_HEAD_DIM
    return pl.pallas_call(
        _retention_kernel,
        grid=(2, nb),
        in_specs=[
            pl.BlockSpec(memory_space=pltpu.SMEM),
            pl.BlockSpec((rows, w), fwd_only), pl.BlockSpec((rows, w), both), pl.BlockSpec((rows, w), both),
            pl.BlockSpec((rows, w), fwd_only), _const_spec((1, w)),
        ],
        out_specs=pl.BlockSpec((rows, w), fwd_only),
        out_shape=jax.ShapeDtypeStruct((s, w), BF16),
        scratch_shapes=[
            pltpu.VMEM((RET_HEADS, hd, hd), F32), pltpu.VMEM((RET_HEADS, hd, hd), F32),
            pltpu.VMEM((nc, RET_HEADS, hd, hd), BF16), pltpu.VMEM((RET_HEADS, c, c), F32),
            pltpu.VMEM((RET_HEADS, c, hd), F32), pltpu.VMEM((RET_HEADS, c, hd), F32),
            pltpu.VMEM((RET_HEADS, c, hd), F32), pltpu.VMEM((RET_HEADS, c, hd), F32),
            pltpu.VMEM((2, RET_HEADS, 8, hd), F32),
        ],
        compiler_params=pltpu.CompilerParams(dimension_semantics=("arbitrary", "arbitrary"),
                                             vmem_limit_bytes=VMEM_LIMIT),
        name="retention",
    )(decay_logit, q, k, v, g, gain)


def _post_kernel(x_ref, o_ref, ry_ref, sga_ref, sgr_ref, p_ref, wao_ref, wro_ref, wout_ref, mg_ref,
                 wup_ref, wdn_ref, pg_ref, wpg_ref, wple_ref, fg_ref, out_ref):
    d = x_ref.shape[1]
    merged = (sga_ref[...].astype(F32) * _dot(o_ref[...], wao_ref[...])
              + sgr_ref[...].astype(F32) * _dot(ry_ref[...], wro_ref[...]))
    x1 = x_ref[...] + _dot(merged.astype(BF16), wout_ref[...])
    hm = _rms(x1, mg_ref[...]).astype(BF16)
    x2 = x1
    for c in range(wup_ref.shape[1] // d):
        sl = slice(c * d, (c + 1) * d)
        u = jnp.maximum(_dot(hm, wup_ref[:, sl]), 0.0)
        x2 = x2 + _dot((u * u).astype(BF16), wdn_ref[sl, :])
    gate = _sigmoid(_dot(_rms(x2, pg_ref[...]).astype(BF16), wpg_ref[...]))
    x3 = x2 + gate * _dot(p_ref[...].astype(BF16), wple_ref[...])
    out_ref[...] = _rms(x3, fg_ref[...])


def _post(x, o, ry, sga, sgr, p, wao, wro, wout, mg, wup, wdn, pg, wpg, wple, fg, tm):
    s, d = x.shape
    row = lambda a: pl.BlockSpec((tm, a.shape[1]), lambda i: (i, 0))
    const = lambda a: _const_spec(a.shape)
    return pl.pallas_call(
        _post_kernel, grid=(s // tm,),
        in_specs=[row(x), row(o), row(ry), row(sga), row(sgr), row(p), const(wao), const(wro), const(wout),
                  const(mg), const(wup), const(wdn), const(pg), const(wpg), const(wple), const(fg)],
        out_specs=pl.BlockSpec((tm, d), lambda i: (i, 0)),
        out_shape=jax.ShapeDtypeStruct((s, d), F32),
        compiler_params=pltpu.CompilerParams(dimension_semantics=("arbitrary",), vmem_limit_bytes=VMEM_LIMIT),
        name="post",
    )(x, o, ry, sga, sgr, p, wao, wro, wout, mg, wup, wdn, pg, wpg, wple, fg)


def _rope_tables(seq_len, head_dim):
    n_axis = head_dim // 4
    lane = jnp.arange(LANES)
    freqs = ROPE_THETA ** (-(lane % n_axis).astype(F32) / n_axis)
    sign = jnp.where((lane % head_dim) < head_dim // 2, -1.0, 1.0).astype(F32)
    rows = jnp.arange(seq_len // GRID_W, dtype=F32)[:, None] * freqs
    cols = jnp.arange(GRID_W, dtype=F32)[:, None] * freqs
    return (jnp.cos(rows), jnp.cos(cols)), (jnp.sin(rows) * sign, jnp.sin(cols) * sign)


def _layer(x, p, mix_norm, w_in, attn_q_norm, attn_k_norm, ret_decay_logit, ret_norm_gain, w_attn_o, w_ret_o,
           w_out, mlp_norm, w_up, w_down, ple_norm, w_ple_gate, w_ple, out_gain, rope_a, rope_r, tk, tq, c, tm):
    reps = LANES // ATTN_HEAD_DIM
    lane = jnp.arange(LANES) // ATTN_HEAD_DIM
    gm = (lane[:, None] == lane[None, :]).astype(BF16)
    row2 = lambda a: a.reshape(1, -1)
    qT, qn2, k, kn2, vT, qr, kr, vr, gr, sga, sgr = _inproj(
        x, row2(mix_norm), w_in.astype(BF16), row2(jnp.tile(attn_q_norm, reps)), row2(jnp.tile(attn_k_norm, reps)),
        rope_a, rope_r, gm, tk)
    o = _attention(qT, qn2, k, vT, kn2, tq)
    ry = _retention(ret_decay_logit, qr, kr, vr, gr, row2(ret_norm_gain), c, min(RET_ROWS, x.shape[0]))
    return _post(x, o, ry, sga, sgr, p, w_attn_o.astype(BF16), w_ret_o.astype(BF16), w_out.astype(BF16),
                 row2(mlp_norm), w_up.astype(BF16), w_down.astype(BF16), row2(ple_norm),
                 w_ple_gate.astype(BF16), w_ple.astype(BF16), row2(out_gain), tm)


def kernel(x, p, mix_norm, w_in, attn_q_norm, attn_k_norm, ret_decay_logit, ret_norm_gain, w_attn_o, w_ret_o,
           w_out, mlp_norm, w_up, w_down, ple_norm, w_ple_gate, w_ple, final_norm):
    b, s, d = x.shape
    depth = p.shape[0]
    assert b == 1 and depth == 1, "single sequence, single layer"
    y = _layer(x[0], p[0, 0], mix_norm[0], w_in[0], attn_q_norm[0], attn_k_norm[0], ret_decay_logit[0],
               ret_norm_gain[0], w_attn_o[0], w_ret_o[0], w_out[0], mlp_norm[0], w_up[0], w_down[0], ple_norm[0],
               w_ple_gate[0], w_ple[0], final_norm, _rope_tables(s, ATTN_HEAD_DIM), _rope_tables(s, RET_HEAD_DIM),
               min(ATTN_K_TILE, s), min(ATTN_Q_TILE, s), min(RET_CHUNK, s), min(ROW_TILE, s))
    return y[None]
```

```python
import jax
import jax.numpy as jnp
from jax import lax
from jax.experimental import pallas as pl
from jax.experimental.pallas import tpu as pltpu

F32 = jnp.float32
BF16 = jnp.bfloat16

GRID_W = 64
ATTN_HEAD_DIM = 64
ATTN_HEADS = 8
ATTN_KV_HEADS = 2
ATTN_GROUP = ATTN_HEADS // ATTN_KV_HEADS
RET_HEAD_DIM = 128
RET_HEADS = 4
ROPE_THETA = 10000.0
NORM_EPS = 1e-6
GN_EPS = 1e-5
LOG2E = 1.4426950408889634
SHIFT_LIMIT = 100.0

LANES = 128
MXU_COLS = 256
BF16_SUBLANES = 16
Q_BIAS_END = ATTN_HEAD_DIM + BF16_SUBLANES
VMEM_LIMIT = 56 * 1024 * 1024

ROW_TILE = 512
WEIGHT_CHUNK = (256, 1024)
ATTN_Q_TILE = 512
ATTN_K_TILE = 512
RET_CHUNK = 256
RET_ROWS = 2048


def _dot(a, b):
    return jnp.dot(a, b, preferred_element_type=F32)


def _dot_nt(a, b):
    return lax.dot_general(a, b, (((1,), (1,)), ((), ())), preferred_element_type=F32)


def _dot_tn(a, b):
    return lax.dot_general(a, b, (((0,), (0,)), ((), ())), preferred_element_type=F32)


def _rms(x, gain):
    return x * lax.rsqrt(jnp.mean(x * x, axis=-1, keepdims=True) + NORM_EPS) * gain


def _sigmoid(x):
    return 1.0 / (1.0 + jnp.exp(-x))


def _const_spec(shape):
    return pl.BlockSpec(shape, lambda *_: (0,) * len(shape), pipeline_mode=pl.Buffered(1))


def _inproj_kernel(x_ref, g_ref, w_ref, qg_ref, kg_ref, rope_a_refs, rope_r_refs, gm_ref,
                   qT_ref, qn2_ref, k_ref, kn2_ref, vT_ref, qr_ref, kr_ref, vr_ref, gr_ref, sga_ref, sgr_ref):
    tm = x_ref.shape[0]
    h = _rms(x_ref[...], g_ref[...]).astype(BF16)
    lane = lax.broadcasted_iota(jnp.int32, (tm, LANES), 1)
    first_half = (lane % ATTN_HEAD_DIM) < (ATTN_HEAD_DIM // 2)
    gm = gm_ref[...]

    def rope_table(rows_ref, cols_ref, head_dim):
        lane64 = lax.broadcasted_iota(jnp.int32, (GRID_W, LANES), 1)
        row_kind = (lane64 % (head_dim // 2)) < (head_dim // 4)
        cols = cols_ref[...]
        return jnp.concatenate([jnp.where(row_kind, jnp.broadcast_to(rows_ref[r:r + 1, :], (GRID_W, LANES)), cols)
                                for r in range(tm // GRID_W)], axis=0)

    ca, sa = (rope_table(rows, cols, ATTN_HEAD_DIM) for rows, cols in rope_a_refs)
    cr, sr = (rope_table(rows, cols, RET_HEAD_DIM) for rows, cols in rope_r_refs)

    def proj_pair(c):
        p = _dot(h, w_ref[:, c * LANES:c * LANES + MXU_COLS])
        return p[:, :LANES], p[:, LANES:]

    def head_norm_rope(a, gain):
        sq = a * a
        hi = sq.astype(BF16)
        lo = (sq - hi.astype(F32)).astype(BF16)
        ss = _dot(hi, gm) + _dot(lo, gm)
        n = a * lax.rsqrt(ss * (1.0 / ATTN_HEAD_DIM) + NORM_EPS) * gain
        partner = jnp.where(first_half, pltpu.roll(n, LANES - ATTN_HEAD_DIM // 2, 1),
                            pltpu.roll(n, ATTN_HEAD_DIM // 2, 1))
        return n * ca + partner * sa

    def ret_rope(a):
        return a * cr + pltpu.roll(a, RET_HEAD_DIM // 2, 1) * sr

    zeros = jnp.zeros((ATTN_HEAD_DIM, tm), BF16)

    def attn_q(c0, slabs):
        for c, a in zip((c0, c0 + 1), slabs):
            q = head_norm_rope(a, qg_ref[...]) * (ATTN_HEAD_DIM ** -0.5 * LOG2E)
            qt = q.T.astype(BF16)
            for j in range(2):
                head = 2 * c + j
                rows = qt[j * ATTN_HEAD_DIM:(j + 1) * ATTN_HEAD_DIM]
                qT_ref[head, 0:ATTN_HEAD_DIM, :] = rows
                qT_ref[head, ATTN_HEAD_DIM:LANES, :] = zeros
                rf = rows.astype(F32)
                qn2_ref[head] = jnp.sum(rf * rf, axis=0, keepdims=True)

    def attn_kv(_, slabs):
        k_slab, v_slab = slabs
        kf = head_norm_rope(k_slab, kg_ref[...]).astype(BF16).astype(F32)
        ksq = kf * kf
        khi = ksq.astype(BF16)
        kss = _dot(khi, gm) + _dot((ksq - khi.astype(F32)).astype(BF16), gm)
        k_sw, kss_sw = pltpu.roll(kf, ATTN_HEAD_DIM, 1), pltpu.roll(kss, ATTN_HEAD_DIM, 1)
        low = lane < ATTN_HEAD_DIM
        one_lane = jnp.where(lane == ATTN_HEAD_DIM, 1.0, 0.0)
        for g in range(ATTN_KV_HEADS):
            k_ref[g] = jnp.where(low, k_sw if g else kf, one_lane).astype(BF16)
            ss_g = jnp.where(low, kss_sw, kss) if g else jnp.where(low, kss, kss_sw)
            kn2_ref[g, 0] = jnp.broadcast_to(jnp.max(ss_g, axis=0, keepdims=True), (8, LANES))
        vt = v_slab.T.astype(BF16)
        for g in range(ATTN_KV_HEADS):
            vT_ref[g, 0] = vt[g * ATTN_HEAD_DIM:(g + 1) * ATTN_HEAD_DIM]

    def plain(out_ref, fn):
        def task(c0, slabs):
            for c, a in zip((c0, c0 + 1), slabs):
                out_ref[:, c * LANES:(c + 1) * LANES] = fn(a).astype(BF16)
        return task

    nd = sga_ref.shape[1] // LANES
    sections = [
        (ATTN_HEADS * ATTN_HEAD_DIM // LANES, attn_q),
        (2, attn_kv),
        (RET_HEADS, plain(qr_ref, lambda a: ret_rope(a) * (RET_HEAD_DIM ** -0.5))),
        (RET_HEADS, plain(kr_ref, ret_rope)),
        (RET_HEADS, plain(vr_ref, lambda a: a)),
        (RET_HEADS, plain(gr_ref, lambda a: a * _sigmoid(a))),
        (nd, plain(sga_ref, _sigmoid)),
        (nd, plain(sgr_ref, _sigmoid)),
    ]
    tasks, col = [], 0
    for count, consumer in sections:
        tasks += [(col + c0, c0, consumer) for c0 in range(0, count, 2)]
        col += count
    pending = proj_pair(tasks[0][0])
    for i, (_, c0, consumer) in enumerate(tasks):
        slabs = pending
        if i + 1 < len(tasks):
            pending = proj_pair(tasks[i + 1][0])
        consumer(c0, slabs)


def _inproj(x, gain, w, qg, kg, rope_a, rope_r, gm, tm):
    s, d = x.shape
    n_in = w.shape[1]
    rw = RET_HEADS * RET_HEAD_DIM
    assert tm % GRID_W == 0
    row = lambda width: pl.BlockSpec((tm, width), lambda i: (i, 0))
    rope_spec = ((pl.BlockSpec((tm // GRID_W, LANES), lambda i: (i, 0)), _const_spec((GRID_W, LANES))),) * 2
    out_shape = (
        jax.ShapeDtypeStruct((ATTN_HEADS, LANES, s), BF16),
        jax.ShapeDtypeStruct((ATTN_HEADS, 1, s), F32),
        jax.ShapeDtypeStruct((ATTN_KV_HEADS, s, LANES), BF16),
        jax.ShapeDtypeStruct((ATTN_KV_HEADS, s // tm, 8, LANES), F32),
        jax.ShapeDtypeStruct((ATTN_KV_HEADS, s // tm, ATTN_HEAD_DIM, tm), BF16),
        jax.ShapeDtypeStruct((s, rw), BF16),
        jax.ShapeDtypeStruct((s, rw), BF16),
        jax.ShapeDtypeStruct((s, rw), BF16),
        jax.ShapeDtypeStruct((s, rw), BF16),
        jax.ShapeDtypeStruct((s, d), BF16),
        jax.ShapeDtypeStruct((s, d), BF16),
    )
    out_specs = (
        pl.BlockSpec((ATTN_HEADS, LANES, tm), lambda i: (0, 0, i)),
        pl.BlockSpec((ATTN_HEADS, 1, tm), lambda i: (0, 0, i)),
        pl.BlockSpec((ATTN_KV_HEADS, tm, LANES), lambda i: (0, i, 0)),
        pl.BlockSpec((ATTN_KV_HEADS, 1, 8, LANES), lambda i: (0, i, 0, 0)),
        pl.BlockSpec((ATTN_KV_HEADS, 1, ATTN_HEAD_DIM, tm), lambda i: (0, i, 0, 0)),
        row(rw), row(rw), row(rw), row(rw), row(d), row(d),
    )
    in_specs = [
        row(d), _const_spec((1, d)), _const_spec((d, n_in)),
        _const_spec((1, LANES)), _const_spec((1, LANES)),
        rope_spec, rope_spec,
        _const_spec((LANES, LANES)),
    ]
    return pl.pallas_call(
        _inproj_kernel, grid=(s // tm,), in_specs=in_specs, out_specs=out_specs, out_shape=out_shape,
        compiler_params=pltpu.CompilerParams(dimension_semantics=("arbitrary",), vmem_limit_bytes=VMEM_LIMIT),
        name="inproj",
    )(x, gain, w, qg, kg, rope_a, rope_r, gm)


def _attn_kernel(qT_ref, qn2_ref, k_ref, vT_ref, kn2_ref, o_ref, qa_scr, acc_scr, l_scr, p_scr, m_scr, s_scr):
    nk, tk = vT_ref.shape[1], vT_ref.shape[3]
    tq = qT_ref.shape[2]
    acc_scr[...] = jnp.zeros(acc_scr.shape, F32)
    l_scr[...] = jnp.zeros(l_scr.shape, F32)

    def key_block(j):
        return k_ref[0, pl.ds(pl.multiple_of(j * tk, tk), tk), :]

    def sublane_sums(p):
        return jnp.sum(p.reshape(p.shape[0] // 8, 8, tq), axis=0)

    kmax2 = jnp.max(jnp.max(kn2_ref[0], axis=0), axis=0, keepdims=True)
    kmax2 = jnp.concatenate([kmax2] * (tq // LANES), axis=1)
    row = lax.broadcasted_iota(jnp.int32, (BF16_SUBLANES, tq), 0)
    bmax = jnp.zeros((1, tq), F32)
    for h in range(ATTN_GROUP):
        b = jnp.sqrt(qn2_ref[h] * kmax2)
        bmax = jnp.maximum(bmax, b)
        qa_scr[h, 0:ATTN_HEAD_DIM, :] = qT_ref[h, 0:ATTN_HEAD_DIM, :]
        qa_scr[h, ATTN_HEAD_DIM:Q_BIAS_END, :] = jnp.where(row == 0, -b, 0.0).astype(BF16)
        qa_scr[h, Q_BIAS_END:LANES, :] = jnp.zeros((LANES - Q_BIAS_END, tq), BF16)
    shift_ok = 2.0 * jnp.max(bmax) <= SHIFT_LIMIT

    @pl.when(shift_ok)
    def _shifted():
        half = tk // 2
        lo, hi = slice(0, half), slice(half, tk)

        def step(j):
            kb, vb = key_block(j), vT_ref[0, j]
            vb_prev = vT_ref[0, jnp.maximum(j - 1, 0)]
            for rows, pv_rows, pv_v in ((lo, hi, vb_prev), (hi, lo, vb)):
                for h in range(ATTN_GROUP):
                    p = jnp.exp2(_dot(kb[rows], qa_scr[h]))
                    l_scr[h] += sublane_sums(p)
                    acc_scr[h] += _dot(pv_v[:, pv_rows], p_scr[h, pv_rows, :])
                    p_scr[h, rows, :] = p.astype(BF16)

        p_scr[:, hi, :] = jnp.zeros((ATTN_GROUP, half, tq), BF16)

        unroll = max(u for u in (1, 2, 4, 8, 16) if nk % u == 0)

        def body(i, carry):
            for u in range(unroll):
                step(unroll * i + u)
            return carry

        lax.fori_loop(0, nk // unroll, body, 0)
        vb = vT_ref[0, nk - 1]
        for h in range(ATTN_GROUP):
            acc_scr[h] += _dot(vb[:, hi], p_scr[h, hi, :])

    @pl.when(jnp.logical_not(shift_ok))
    def _online():
        m_scr[...] = jnp.full(m_scr.shape, -jnp.inf, F32)

        def scores(j, buf):
            kb = key_block(j)
            for h in range(ATTN_GROUP):
                s_scr[buf, h] = _dot(kb, qT_ref[h])

        def update(j, buf):
            vb = vT_ref[0, j]
            ps, alphas = [], []
            for h in range(ATTN_GROUP):
                s = s_scr[buf, h]
                m_prev = m_scr[h]
                m_new = jnp.maximum(m_prev, jnp.max(s, axis=0, keepdims=True))
                p = jnp.exp2(s - m_new)
                alpha = jnp.exp2(m_prev - m_new)
                l_scr[h] = l_scr[h] * alpha + sublane_sums(p)
                ps.append(p.astype(BF16))
                alphas.append(alpha)
                m_scr[h] = m_new
            for h in range(ATTN_GROUP):
                acc_scr[h] = acc_scr[h] * alphas[h] + _dot(vb, ps[h])

        scores(0, 0)

        def body(i, carry):
            j = 2 * i
            scores(j + 1, 1)
            update(j, 0)
            scores(jnp.minimum(j + 2, nk - 1), 0)
            update(j + 1, 1)
            return carry

        lax.fori_loop(0, nk // 2, body, 0)

    outs = [acc_scr[h] * (1.0 / jnp.sum(l_scr[h], axis=0, keepdims=True)) for h in range(ATTN_GROUP)]
    o_ref[...] = jnp.concatenate(outs, axis=0).T.astype(o_ref.dtype)


def _attention(qT, qn2, k, vT, kn2, tq):
    s = k.shape[1]
    nk, tk = vT.shape[1], vT.shape[3]
    assert nk % 2 == 0 and tq % LANES == 0
    gw = ATTN_GROUP * ATTN_HEAD_DIM
    return pl.pallas_call(
        _attn_kernel,
        grid=(ATTN_KV_HEADS, s // tq),
        in_specs=[
            pl.BlockSpec((ATTN_GROUP, LANES, tq), lambda g, i: (g, 0, i)),
            pl.BlockSpec((ATTN_GROUP, 1, tq), lambda g, i: (g, 0, i)),
            pl.BlockSpec((1, s, LANES), lambda g, i: (g, 0, 0)),
            pl.BlockSpec((1, nk, ATTN_HEAD_DIM, tk), lambda g, i: (g, 0, 0, 0)),
            pl.BlockSpec((1, nk, 8, LANES), lambda g, i: (g, 0, 0, 0)),
        ],
        out_specs=pl.BlockSpec((tq, gw), lambda g, i: (i, g)),
        out_shape=jax.ShapeDtypeStruct((s, ATTN_HEADS * ATTN_HEAD_DIM), BF16),
        scratch_shapes=[pltpu.VMEM((ATTN_GROUP, LANES, tq), BF16), pltpu.VMEM((ATTN_GROUP, ATTN_HEAD_DIM, tq), F32),
                        pltpu.VMEM((ATTN_GROUP, 8, tq), F32),
                        pltpu.VMEM((ATTN_GROUP, tk, tq), BF16), pltpu.VMEM((ATTN_GROUP, 1, tq), F32),
                        pltpu.VMEM((2, ATTN_GROUP, tk, tq), F32)],
        compiler_params=pltpu.CompilerParams(dimension_semantics=("arbitrary", "arbitrary"),
                                             vmem_limit_bytes=VMEM_LIMIT),
        name="attention",
    )(qT, qn2, k, vT, kn2)


def _retention_kernel(dl_ref, q_ref, k_ref, v_ref, g_ref, gain_ref, o_ref,
                      sf_scr, sb_scr, sball_scr, dmask_scr, kdf_scr, kdb_scr, qdf_scr, qdb_scr, cd_scr):
    ph, n = pl.program_id(0), pl.program_id(1)
    nb = pl.num_programs(1)
    c = dmask_scr.shape[1]
    nsub = q_ref.shape[0] // c
    hd = RET_HEAD_DIM

    @pl.when((ph == 0) & (n == 0))
    def _init():
        sf_scr[...] = jnp.zeros(sf_scr.shape, F32)
        sb_scr[...] = jnp.zeros(sb_scr.shape, F32)
        pos = lax.broadcasted_iota(jnp.int32, (c, hd), 0).astype(F32)
        ii = lax.broadcasted_iota(jnp.int32, (c, c), 0)
        jj = lax.broadcasted_iota(jnp.int32, (c, c), 1)
        diff = (ii - jj).astype(F32)

        def log_sigmoid(shape, x):
            v = jnp.full(shape, x, F32)
            return -(jnp.maximum(-v, 0.0) + jnp.log(1.0 + jnp.exp(-jnp.abs(v))))

        for h in range(RET_HEADS):
            lgf, lgb = log_sigmoid((c, hd), dl_ref[0, h]), log_sigmoid((c, hd), dl_ref[1, h])
            kdf_scr[h] = jnp.exp(lgf * (c - 1.0 - pos))
            qdf_scr[h] = jnp.exp(lgf * (pos + 1.0))
            kdb_scr[h] = jnp.exp(lgb * pos)
            qdb_scr[h] = jnp.exp(lgb * (c - pos))
            cd_scr[0, h] = jnp.exp(lgf[0:8] * float(c))
            cd_scr[1, h] = jnp.exp(lgb[0:8] * float(c))
            lgf2, lgb2 = log_sigmoid((c, c), dl_ref[0, h]), log_sigmoid((c, c), dl_ref[1, h])
            dmask_scr[h] = jnp.where(diff >= 0, jnp.exp(lgf2 * jnp.maximum(diff, 0.0)),
                                     jnp.exp(lgb2 * jnp.maximum(-diff, 0.0)))

    @pl.when(ph == 0)
    def _backward_states():
        for sub in reversed(range(nsub)):
            rows = slice(sub * c, (sub + 1) * c)
            m = (nb - 1 - n) * nsub + sub
            sball_scr[m] = sb_scr[...].astype(BF16)
            for h in range(RET_HEADS):
                sl = slice(h * hd, (h + 1) * hd)
                kd = (k_ref[rows, sl].astype(F32) * kdb_scr[h]).astype(BF16)
                sb_scr[h] = sb_scr[h] * cd_scr[1, h, 0:1, :] + _dot_tn(kd, v_ref[rows, sl])

    @pl.when(ph == 1)
    def _forward():
        for sub in range(nsub):
            rows = slice(sub * c, (sub + 1) * c)
            m = n * nsub + sub
            for h in range(RET_HEADS):
                sl = slice(h * hd, (h + 1) * hd)
                q, k, v = q_ref[rows, sl], k_ref[rows, sl], v_ref[rows, sl]
                qf, kf = q.astype(F32), k.astype(F32)
                a = (_dot_nt(q, k) * dmask_scr[h]).astype(BF16)
                y = _dot(a, v)
                y += _dot((qf * qdf_scr[h]).astype(BF16), sf_scr[h].astype(BF16))
                y += _dot((qf * qdb_scr[h]).astype(BF16), sball_scr[m, h])
                sf_scr[h] = sf_scr[h] * cd_scr[0, h, 0:1, :] + _dot_tn((kf * kdf_scr[h]).astype(BF16), v)
                mu = jnp.mean(y, axis=-1, keepdims=True)
                d = y - mu
                var = jnp.mean(d * d, axis=-1, keepdims=True)
                yn = d * lax.rsqrt(var + GN_EPS)
                o_ref[rows, sl] = (yn * gain_ref[:, sl] * g_ref[rows, sl].astype(F32)).astype(o_ref.dtype)


def _retention(decay_logit, q, k, v, g, gain, c, rows):
    s, w = q.shape
    nc, nb = s // c, s // rows
    assert rows % c == 0
    fwd_only = lambda ph, n: (ph * n, 0)
    both = lambda ph, n: (ph * n + (1 - ph) * (nb - 1 - n), 0)
    hd = RET_HEAD_DIM
    return pl.pallas_call(
        _retention_kernel,
        grid=(2, nb),
        in_specs=[
            pl.BlockSpec(memory_space=pltpu.SMEM),
            pl.BlockSpec((rows, w), fwd_only), pl.BlockSpec((rows, w), both), pl.BlockSpec((rows, w), both),
            pl.BlockSpec((rows, w), fwd_only), _const_spec((1, w)),
        ],
        out_specs=pl.BlockSpec((rows, w), fwd_only),
        out_shape=jax.ShapeDtypeStruct((s, w), BF16),
        scratch_shapes=[
            pltpu.VMEM((RET_HEADS, hd, hd), F32), pltpu.VMEM((RET_HEADS, hd, hd), F32),
            pltpu.VMEM((nc, RET_HEADS, hd, hd), BF16), pltpu.VMEM((RET_HEADS, c, c), F32),
            pltpu.VMEM((RET_HEADS, c, hd), F32), pltpu.VMEM((RET_HEADS, c, hd), F32),
            pltpu.VMEM((RET_HEADS, c, hd), F32), pltpu.VMEM((RET_HEADS, c, hd), F32),
            pltpu.VMEM((2, RET_HEADS, 8, hd), F32),
        ],
        compiler_params=pltpu.CompilerParams(dimension_semantics=("arbitrary", "arbitrary"),
                                             vmem_limit_bytes=VMEM_LIMIT),
        name="retention",
    )(decay_logit, q, k, v, g, gain)


def _stage_weights(pairs, stage, sem):
    rows, cols = stage.shape[1:]
    chunks = [(src.at[pl.ds(r0, rows), pl.ds(c0, cols)], dst, r0, c0)
              for src, dst in pairs
              for c0 in range(0, src.shape[1], cols) for r0 in range(0, src.shape[0], rows)]
    copy = lambda i: pltpu.make_async_copy(chunks[i][0], stage.at[i % 2], sem.at[i % 2])
    copy(0).start()
    for i, (_, dst, r0, c0) in enumerate(chunks):
        if i + 1 < len(chunks):
            copy(i + 1).start()
        copy(i).wait()
        dst[r0:r0 + rows, c0:c0 + cols] = stage[i % 2].astype(BF16)


def _post_kernel(x_ref, o_ref, ry_ref, sga_ref, sgr_ref, p_ref, wao_hbm, wro_hbm, wout_hbm, mg_ref,
                 wup_hbm, wdn_hbm, pg_ref, wpg_hbm, wple_hbm, fg_ref, out_ref,
                 wao_ref, wro_ref, wout_ref, wup_ref, wdn_ref, wpg_ref, wple_ref, stage, sem):
    @pl.when(pl.program_id(0) == 0)
    def _load_weights():
        _stage_weights([(wao_hbm, wao_ref), (wro_hbm, wro_ref), (wout_hbm, wout_ref), (wup_hbm, wup_ref),
                        (wdn_hbm, wdn_ref), (wpg_hbm, wpg_ref), (wple_hbm, wple_ref)], stage, sem)

    d = x_ref.shape[1]
    merged = (sga_ref[...].astype(F32) * _dot(o_ref[...], wao_ref[...])
              + sgr_ref[...].astype(F32) * _dot(ry_ref[...], wro_ref[...]))
    x1 = x_ref[...] + _dot(merged.astype(BF16), wout_ref[...])
    hm = _rms(x1, mg_ref[...]).astype(BF16)
    x2 = x1
    for c in range(wup_ref.shape[1] // d):
        sl = slice(c * d, (c + 1) * d)
        u = jnp.maximum(_dot(hm, wup_ref[:, sl]), 0.0)
        x2 = x2 + _dot((u * u).astype(BF16), wdn_ref[sl, :])
    gate = _sigmoid(_dot(_rms(x2, pg_ref[...]).astype(BF16), wpg_ref[...]))
    x3 = x2 + gate * _dot(p_ref[...].astype(BF16), wple_ref[...])
    out_ref[...] = _rms(x3, fg_ref[...])


def _post(x, o, ry, sga, sgr, p, wao, wro, wout, mg, wup, wdn, pg, wpg, wple, fg, tm):
    s, d = x.shape
    row = lambda a: pl.BlockSpec((tm, a.shape[1]), lambda i: (i, 0))
    const = lambda a: _const_spec(a.shape)
    hbm = lambda a: pl.BlockSpec(memory_space=pl.ANY)
    weights = (wao, wro, wout, wup, wdn, wpg, wple)
    assert all(w.shape[0] % WEIGHT_CHUNK[0] == 0 and w.shape[1] % WEIGHT_CHUNK[1] == 0 for w in weights)
    return pl.pallas_call(
        _post_kernel, grid=(s // tm,),
        in_specs=[row(x), row(o), row(ry), row(sga), row(sgr), row(p), hbm(wao), hbm(wro), hbm(wout),
                  const(mg), hbm(wup), hbm(wdn), const(pg), hbm(wpg), hbm(wple), const(fg)],
        out_specs=pl.BlockSpec((tm, d), lambda i: (i, 0)),
        out_shape=jax.ShapeDtypeStruct((s, d), F32),
        scratch_shapes=[pltpu.VMEM(w.shape, BF16) for w in weights]
        + [pltpu.VMEM((2,) + WEIGHT_CHUNK, F32), pltpu.SemaphoreType.DMA((2,))],
        compiler_params=pltpu.CompilerParams(dimension_semantics=("arbitrary",), vmem_limit_bytes=VMEM_LIMIT),
        name="post",
    )(x, o, ry, sga, sgr, p, wao, wro, wout, mg, wup, wdn, pg, wpg, wple, fg)


def _rope_tables(seq_len, head_dim):
    n_axis = head_dim // 4
    lane = jnp.arange(LANES)
    freqs = ROPE_THETA ** (-(lane % n_axis).astype(F32) / n_axis)
    sign = jnp.where((lane % head_dim) < head_dim // 2, -1.0, 1.0).astype(F32)
    rows = jnp.arange(seq_len // GRID_W, dtype=F32)[:, None] * freqs
    cols = jnp.arange(GRID_W, dtype=F32)[:, None] * freqs
    return (jnp.cos(rows), jnp.cos(cols)), (jnp.sin(rows) * sign, jnp.sin(cols) * sign)


def _layer(x, p, mix_norm, w_in, attn_q_norm, attn_k_norm, ret_decay_logit, ret_norm_gain, w_attn_o, w_ret_o,
           w_out, mlp_norm, w_up, w_down, ple_norm, w_ple_gate, w_ple, out_gain, rope_a, rope_r, tk, tq, c, tm):
    reps = LANES // ATTN_HEAD_DIM
    lane = jnp.arange(LANES) // ATTN_HEAD_DIM
    gm = (lane[:, None] == lane[None, :]).astype(BF16)
    row2 = lambda a: a.reshape(1, -1)
    qT, qn2, k, kn2, vT, qr, kr, vr, gr, sga, sgr = _inproj(
        x, row2(mix_norm), w_in.astype(BF16), row2(jnp.tile(attn_q_norm, reps)), row2(jnp.tile(attn_k_norm, reps)),
        rope_a, rope_r, gm, tk)
    o = _attention(qT, qn2, k, vT, kn2, tq)
    ry = _retention(ret_decay_logit, qr, kr, vr, gr, row2(ret_norm_gain), c, min(RET_ROWS, x.shape[0]))
    return _post(x, o, ry, sga, sgr, p, w_attn_o, w_ret_o, w_out, row2(mlp_norm), w_up, w_down, row2(ple_norm),
                 w_ple_gate, w_ple, row2(out_gain), tm)


def kernel(x, p, mix_norm, w_in, attn_q_norm, attn_k_norm, ret_decay_logit, ret_norm_gain, w_attn_o, w_ret_o,
           w_out, mlp_norm, w_up, w_down, ple_norm, w_ple_gate, w_ple, final_norm):
    b, s, d = x.shape
    depth = p.shape[0]
    assert b == 1 and depth == 1, "single sequence, single layer"
    y = _layer(x[0], p[0, 0], mix_norm[0], w_in[0], attn_q_norm[0], attn_k_norm[0], ret_decay_logit[0],
               ret_norm_gain[0], w_attn_o[0], w_ret_o[0], w_out[0], mlp_norm[0], w_up[0], w_down[0], ple_norm[0],
               w_ple_gate[0], w_ple[0], final_norm, _rope_tables(s, ATTN_HEAD_DIM), _rope_tables(s, RET_HEAD_DIM),
               min(ATTN_K_TILE, s), min(ATTN_Q_TILE, s), min(RET_CHUNK, s), min(ROW_TILE, s))
    return y[None]
```

```python
import jax
import jax.numpy as jnp
from jax import lax
from jax.experimental import pallas as pl
from jax.experimental.pallas import tpu as pltpu

F32 = jnp.float32
BF16 = jnp.bfloat16

GRID_W = 64
ATTN_HEAD_DIM = 64
ATTN_HEADS = 8
ATTN_KV_HEADS = 2
ATTN_GROUP = ATTN_HEADS // ATTN_KV_HEADS
RET_HEAD_DIM = 128
RET_HEADS = 4
ROPE_THETA = 10000.0
NORM_EPS = 1e-6
GN_EPS = 1e-5
LOG2E = 1.4426950408889634
SHIFT_LIMIT = 100.0

LANES = 128
MXU_COLS = 256
BF16_SUBLANES = 16
Q_BIAS_END = ATTN_HEAD_DIM + BF16_SUBLANES
VMEM_LIMIT = 56 * 1024 * 1024

ROW_TILE = 512
WEIGHT_CHUNK = (256, 1024)
WEIGHT_SLOTS = 4
ATTN_Q_TILE = 512
ATTN_K_TILE = 512
RET_CHUNK = 256
RET_ROWS = 2048


def _dot(a, b):
    return jnp.dot(a, b, preferred_element_type=F32)


def _dot_nt(a, b):
    return lax.dot_general(a, b, (((1,), (1,)), ((), ())), preferred_element_type=F32)


def _dot_tn(a, b):
    return lax.dot_general(a, b, (((0,), (0,)), ((), ())), preferred_element_type=F32)


def _rms(x, gain):
    return x * lax.rsqrt(jnp.mean(x * x, axis=-1, keepdims=True) + NORM_EPS) * gain


def _sigmoid(x):
    return 1.0 / (1.0 + jnp.exp(-x))


def _const_spec(shape):
    return pl.BlockSpec(shape, lambda *_: (0,) * len(shape), pipeline_mode=pl.Buffered(1))


def _inproj_kernel(x_ref, g_ref, w_ref, qg_ref, kg_ref, rope_a_refs, rope_r_refs, gm_ref,
                   qT_ref, qn2_ref, k_ref, kn2_ref, vT_ref, qr_ref, kr_ref, vr_ref, gr_ref, sga_ref, sgr_ref):
    tm = x_ref.shape[0]
    h = _rms(x_ref[...], g_ref[...]).astype(BF16)
    lane = lax.broadcasted_iota(jnp.int32, (tm, LANES), 1)
    first_half = (lane % ATTN_HEAD_DIM) < (ATTN_HEAD_DIM // 2)
    gm = gm_ref[...]

    def rope_table(rows_ref, cols_ref, head_dim):
        lane64 = lax.broadcasted_iota(jnp.int32, (GRID_W, LANES), 1)
        row_kind = (lane64 % (head_dim // 2)) < (head_dim // 4)
        cols = cols_ref[...]
        return jnp.concatenate([jnp.where(row_kind, jnp.broadcast_to(rows_ref[r:r + 1, :], (GRID_W, LANES)), cols)
                                for r in range(tm // GRID_W)], axis=0)

    ca, sa = (rope_table(rows, cols, ATTN_HEAD_DIM) for rows, cols in rope_a_refs)
    cr, sr = (rope_table(rows, cols, RET_HEAD_DIM) for rows, cols in rope_r_refs)

    def proj_pair(c):
        p = _dot(h, w_ref[:, c * LANES:c * LANES + MXU_COLS])
        return p[:, :LANES], p[:, LANES:]

    def head_norm_rope(a, gain):
        sq = a * a
        hi = sq.astype(BF16)
        lo = (sq - hi.astype(F32)).astype(BF16)
        ss = _dot(hi, gm) + _dot(lo, gm)
        n = a * lax.rsqrt(ss * (1.0 / ATTN_HEAD_DIM) + NORM_EPS) * gain
        partner = jnp.where(first_half, pltpu.roll(n, LANES - ATTN_HEAD_DIM // 2, 1),
                            pltpu.roll(n, ATTN_HEAD_DIM // 2, 1))
        return n * ca + partner * sa

    def ret_rope(a):
        return a * cr + pltpu.roll(a, RET_HEAD_DIM // 2, 1) * sr

    zeros = jnp.zeros((ATTN_HEAD_DIM, tm), BF16)

    def attn_q(c0, slabs):
        for c, a in zip((c0, c0 + 1), slabs):
            q = head_norm_rope(a, qg_ref[...]) * (ATTN_HEAD_DIM ** -0.5 * LOG2E)
            qt = q.T.astype(BF16)
            for j in range(2):
                head = 2 * c + j
                rows = qt[j * ATTN_HEAD_DIM:(j + 1) * ATTN_HEAD_DIM]
                qT_ref[head, 0:ATTN_HEAD_DIM, :] = rows
                qT_ref[head, ATTN_HEAD_DIM:LANES, :] = zeros
                rf = rows.astype(F32)
                qn2_ref[head] = jnp.sum(rf * rf, axis=0, keepdims=True)

    def attn_kv(_, slabs):
        k_slab, v_slab = slabs
        kf = head_norm_rope(k_slab, kg_ref[...]).astype(BF16).astype(F32)
        ksq = kf * kf
        khi = ksq.astype(BF16)
        kss = _dot(khi, gm) + _dot((ksq - khi.astype(F32)).astype(BF16), gm)
        k_sw, kss_sw = pltpu.roll(kf, ATTN_HEAD_DIM, 1), pltpu.roll(kss, ATTN_HEAD_DIM, 1)
        low = lane < ATTN_HEAD_DIM
        one_lane = jnp.where(lane == ATTN_HEAD_DIM, 1.0, 0.0)
        for g in range(ATTN_KV_HEADS):
            k_ref[g] = jnp.where(low, k_sw if g else kf, one_lane).astype(BF16)
            ss_g = jnp.where(low, kss_sw, kss) if g else jnp.where(low, kss, kss_sw)
            kn2_ref[g, 0] = jnp.broadcast_to(jnp.max(ss_g, axis=0, keepdims=True), (8, LANES))
        vt = v_slab.T.astype(BF16)
        for g in range(ATTN_KV_HEADS):
            vT_ref[g, 0] = vt[g * ATTN_HEAD_DIM:(g + 1) * ATTN_HEAD_DIM]

    def plain(out_ref, fn):
        def task(c0, slabs):
            for c, a in zip((c0, c0 + 1), slabs):
                out_ref[:, c * LANES:(c + 1) * LANES] = fn(a).astype(BF16)
        return task

    nd = sga_ref.shape[1] // LANES
    sections = [
        (ATTN_HEADS * ATTN_HEAD_DIM // LANES, attn_q),
        (2, attn_kv),
        (RET_HEADS, plain(qr_ref, lambda a: ret_rope(a) * (RET_HEAD_DIM ** -0.5))),
        (RET_HEADS, plain(kr_ref, ret_rope)),
        (RET_HEADS, plain(vr_ref, lambda a: a)),
        (RET_HEADS, plain(gr_ref, lambda a: a * _sigmoid(a))),
        (nd, plain(sga_ref, _sigmoid)),
        (nd, plain(sgr_ref, _sigmoid)),
    ]
    tasks, col = [], 0
    for count, consumer in sections:
        tasks += [(col + c0, c0, consumer) for c0 in range(0, count, 2)]
        col += count
    pending = proj_pair(tasks[0][0])
    for i, (_, c0, consumer) in enumerate(tasks):
        slabs = pending
        if i + 1 < len(tasks):
            pending = proj_pair(tasks[i + 1][0])
        consumer(c0, slabs)


def _inproj(x, gain, w, qg, kg, rope_a, rope_r, gm, tm):
    s, d = x.shape
    n_in = w.shape[1]
    rw = RET_HEADS * RET_HEAD_DIM
    assert tm % GRID_W == 0
    row = lambda width: pl.BlockSpec((tm, width), lambda i: (i, 0))
    rope_spec = ((pl.BlockSpec((tm // GRID_W, LANES), lambda i: (i, 0)), _const_spec((GRID_W, LANES))),) * 2
    out_shape = (
        jax.ShapeDtypeStruct((ATTN_HEADS, LANES, s), BF16),
        jax.ShapeDtypeStruct((ATTN_HEADS, 1, s), F32),
        jax.ShapeDtypeStruct((ATTN_KV_HEADS, s, LANES), BF16),
        jax.ShapeDtypeStruct((ATTN_KV_HEADS, s // tm, 8, LANES), F32),
        jax.ShapeDtypeStruct((ATTN_KV_HEADS, s // tm, ATTN_HEAD_DIM, tm), BF16),
        jax.ShapeDtypeStruct((s, rw), BF16),
        jax.ShapeDtypeStruct((s, rw), BF16),
        jax.ShapeDtypeStruct((s, rw), BF16),
        jax.ShapeDtypeStruct((s, rw), BF16),
        jax.ShapeDtypeStruct((s, d), BF16),
        jax.ShapeDtypeStruct((s, d), BF16),
    )
    out_specs = (
        pl.BlockSpec((ATTN_HEADS, LANES, tm), lambda i: (0, 0, i)),
        pl.BlockSpec((ATTN_HEADS, 1, tm), lambda i: (0, 0, i)),
        pl.BlockSpec((ATTN_KV_HEADS, tm, LANES), lambda i: (0, i, 0)),
        pl.BlockSpec((ATTN_KV_HEADS, 1, 8, LANES), lambda i: (0, i, 0, 0)),
        pl.BlockSpec((ATTN_KV_HEADS, 1, ATTN_HEAD_DIM, tm), lambda i: (0, i, 0, 0)),
        row(rw), row(rw), row(rw), row(rw), row(d), row(d),
    )
    in_specs = [
        row(d), _const_spec((1, d)), _const_spec((d, n_in)),
        _const_spec((1, LANES)), _const_spec((1, LANES)),
        rope_spec, rope_spec,
        _const_spec((LANES, LANES)),
    ]
    return pl.pallas_call(
        _inproj_kernel, grid=(s // tm,), in_specs=in_specs, out_specs=out_specs, out_shape=out_shape,
        compiler_params=pltpu.CompilerParams(dimension_semantics=("arbitrary",), vmem_limit_bytes=VMEM_LIMIT),
        name="inproj",
    )(x, gain, w, qg, kg, rope_a, rope_r, gm)


def _attn_kernel(qT_ref, qn2_ref, k_ref, vT_ref, kn2_ref, o_ref, qa_scr, acc_scr, l_scr, p_scr, m_scr, s_scr):
    nk, tk = vT_ref.shape[1], vT_ref.shape[3]
    tq = qT_ref.shape[2]
    acc_scr[...] = jnp.zeros(acc_scr.shape, F32)
    l_scr[...] = jnp.zeros(l_scr.shape, F32)

    def key_block(j):
        return k_ref[0, pl.ds(pl.multiple_of(j * tk, tk), tk), :]

    def sublane_sums(p):
        return jnp.sum(p.reshape(p.shape[0] // 8, 8, tq), axis=0)

    kmax2 = jnp.max(jnp.max(kn2_ref[0], axis=0), axis=0, keepdims=True)
    kmax2 = jnp.concatenate([kmax2] * (tq // LANES), axis=1)
    row = lax.broadcasted_iota(jnp.int32, (BF16_SUBLANES, tq), 0)
    bmax = jnp.zeros((1, tq), F32)
    for h in range(ATTN_GROUP):
        b = jnp.sqrt(qn2_ref[h] * kmax2)
        bmax = jnp.maximum(bmax, b)
        qa_scr[h, 0:ATTN_HEAD_DIM, :] = qT_ref[h, 0:ATTN_HEAD_DIM, :]
        qa_scr[h, ATTN_HEAD_DIM:Q_BIAS_END, :] = jnp.where(row == 0, -b, 0.0).astype(BF16)
        qa_scr[h, Q_BIAS_END:LANES, :] = jnp.zeros((LANES - Q_BIAS_END, tq), BF16)
    shift_ok = 2.0 * jnp.max(bmax) <= SHIFT_LIMIT

    @pl.when(shift_ok)
    def _shifted():
        half = tk // 2
        lo, hi = slice(0, half), slice(half, tk)

        def step(j):
            kb, vb = key_block(j), vT_ref[0, j]
            vb_prev = vT_ref[0, jnp.maximum(j - 1, 0)]
            for rows, pv_rows, pv_v in ((lo, hi, vb_prev), (hi, lo, vb)):
                for h in range(ATTN_GROUP):
                    p = jnp.exp2(_dot(kb[rows], qa_scr[h]))
                    l_scr[h] += sublane_sums(p)
                    acc_scr[h] += _dot(pv_v[:, pv_rows], p_scr[h, pv_rows, :])
                    p_scr[h, rows, :] = p.astype(BF16)

        p_scr[:, hi, :] = jnp.zeros((ATTN_GROUP, half, tq), BF16)

        unroll = max(u for u in (1, 2, 4, 8, 16) if nk % u == 0)

        def body(i, carry):
            for u in range(unroll):
                step(unroll * i + u)
            return carry

        lax.fori_loop(0, nk // unroll, body, 0)
        vb = vT_ref[0, nk - 1]
        for h in range(ATTN_GROUP):
            acc_scr[h] += _dot(vb[:, hi], p_scr[h, hi, :])

    @pl.when(jnp.logical_not(shift_ok))
    def _online():
        m_scr[...] = jnp.full(m_scr.shape, -jnp.inf, F32)

        def scores(j, buf):
            kb = key_block(j)
            for h in range(ATTN_GROUP):
                s_scr[buf, h] = _dot(kb, qT_ref[h])

        def update(j, buf):
            vb = vT_ref[0, j]
            ps, alphas = [], []
            for h in range(ATTN_GROUP):
                s = s_scr[buf, h]
                m_prev = m_scr[h]
                m_new = jnp.maximum(m_prev, jnp.max(s, axis=0, keepdims=True))
                p = jnp.exp2(s - m_new)
                alpha = jnp.exp2(m_prev - m_new)
                l_scr[h] = l_scr[h] * alpha + sublane_sums(p)
                ps.append(p.astype(BF16))
                alphas.append(alpha)
                m_scr[h] = m_new
            for h in range(ATTN_GROUP):
                acc_scr[h] = acc_scr[h] * alphas[h] + _dot(vb, ps[h])

        scores(0, 0)

        def body(i, carry):
            j = 2 * i
            scores(j + 1, 1)
            update(j, 0)
            scores(jnp.minimum(j + 2, nk - 1), 0)
            update(j + 1, 1)
            return carry

        lax.fori_loop(0, nk // 2, body, 0)

    outs = [acc_scr[h] * (1.0 / jnp.sum(l_scr[h], axis=0, keepdims=True)) for h in range(ATTN_GROUP)]
    o_ref[...] = jnp.concatenate(outs, axis=0).T.astype(o_ref.dtype)


def _attention(qT, qn2, k, vT, kn2, tq):
    s = k.shape[1]
    nk, tk = vT.shape[1], vT.shape[3]
    assert nk % 2 == 0 and tq % LANES == 0
    gw = ATTN_GROUP * ATTN_HEAD_DIM
    return pl.pallas_call(
        _attn_kernel,
        grid=(ATTN_KV_HEADS, s // tq),
        in_specs=[
            pl.BlockSpec((ATTN_GROUP, LANES, tq), lambda g, i: (g, 0, i)),
            pl.BlockSpec((ATTN_GROUP, 1, tq), lambda g, i: (g, 0, i)),
            pl.BlockSpec((1, s, LANES), lambda g, i: (g, 0, 0)),
            pl.BlockSpec((1, nk, ATTN_HEAD_DIM, tk), lambda g, i: (g, 0, 0, 0)),
            pl.BlockSpec((1, nk, 8, LANES), lambda g, i: (g, 0, 0, 0)),
        ],
        out_specs=pl.BlockSpec((tq, gw), lambda g, i: (i, g)),
        out_shape=jax.ShapeDtypeStruct((s, ATTN_HEADS * ATTN_HEAD_DIM), BF16),
        scratch_shapes=[pltpu.VMEM((ATTN_GROUP, LANES, tq), BF16), pltpu.VMEM((ATTN_GROUP, ATTN_HEAD_DIM, tq), F32),
                        pltpu.VMEM((ATTN_GROUP, 8, tq), F32),
                        pltpu.VMEM((ATTN_GROUP, tk, tq), BF16), pltpu.VMEM((ATTN_GROUP, 1, tq), F32),
                        pltpu.VMEM((2, ATTN_GROUP, tk, tq), F32)],
        compiler_params=pltpu.CompilerParams(dimension_semantics=("arbitrary", "arbitrary"),
                                             vmem_limit_bytes=VMEM_LIMIT),
        name="attention",
    )(qT, qn2, k, vT, kn2)


def _retention_kernel(dl_ref, q_ref, k_ref, v_ref, g_ref, gain_ref, o_ref,
                      sf_scr, sb_scr, sball_scr, dmask_scr, kdf_scr, kdb_scr, qdf_scr, qdb_scr, cd_scr):
    ph, n = pl.program_id(0), pl.program_id(1)
    nb = pl.num_programs(1)
    c = dmask_scr.shape[1]
    nsub = q_ref.shape[0] // c
    hd = RET_HEAD_DIM

    @pl.when((ph == 0) & (n == 0))
    def _init():
        sf_scr[...] = jnp.zeros(sf_scr.shape, F32)
        sb_scr[...] = jnp.zeros(sb_scr.shape, F32)
        pos = lax.broadcasted_iota(jnp.int32, (c, hd), 0).astype(F32)
        ii = lax.broadcasted_iota(jnp.int32, (c, c), 0)
        jj = lax.broadcasted_iota(jnp.int32, (c, c), 1)
        diff = (ii - jj).astype(F32)

        def log_sigmoid(shape, x):
            v = jnp.full(shape, x, F32)
            return -(jnp.maximum(-v, 0.0) + jnp.log(1.0 + jnp.exp(-jnp.abs(v))))

        for h in range(RET_HEADS):
            lgf, lgb = log_sigmoid((c, hd), dl_ref[0, h]), log_sigmoid((c, hd), dl_ref[1, h])
            kdf_scr[h] = jnp.exp(lgf * (c - 1.0 - pos))
            qdf_scr[h] = jnp.exp(lgf * (pos + 1.0))
            kdb_scr[h] = jnp.exp(lgb * pos)
            qdb_scr[h] = jnp.exp(lgb * (c - pos))
            cd_scr[0, h] = jnp.exp(lgf[0:8] * float(c))
            cd_scr[1, h] = jnp.exp(lgb[0:8] * float(c))
            lgf2, lgb2 = log_sigmoid((c, c), dl_ref[0, h]), log_sigmoid((c, c), dl_ref[1, h])
            dmask_scr[h] = jnp.where(diff >= 0, jnp.exp(lgf2 * jnp.maximum(diff, 0.0)),
                                     jnp.exp(lgb2 * jnp.maximum(-diff, 0.0)))

    @pl.when(ph == 0)
    def _backward_states():
        for sub in reversed(range(nsub)):
            rows = slice(sub * c, (sub + 1) * c)
            m = (nb - 1 - n) * nsub + sub
            sball_scr[m] = sb_scr[...].astype(BF16)
            for h in range(RET_HEADS):
                sl = slice(h * hd, (h + 1) * hd)
                kd = (k_ref[rows, sl].astype(F32) * kdb_scr[h]).astype(BF16)
                sb_scr[h] = sb_scr[h] * cd_scr[1, h, 0:1, :] + _dot_tn(kd, v_ref[rows, sl])

    @pl.when(ph == 1)
    def _forward():
        for sub in range(nsub):
            rows = slice(sub * c, (sub + 1) * c)
            m = n * nsub + sub
            for h in range(RET_HEADS):
                sl = slice(h * hd, (h + 1) * hd)
                q, k, v = q_ref[rows, sl], k_ref[rows, sl], v_ref[rows, sl]
                qf, kf = q.astype(F32), k.astype(F32)
                a = (_dot_nt(q, k) * dmask_scr[h]).astype(BF16)
                y = _dot(a, v)
                y += _dot((qf * qdf_scr[h]).astype(BF16), sf_scr[h].astype(BF16))
                y += _dot((qf * qdb_scr[h]).astype(BF16), sball_scr[m, h])
                sf_scr[h] = sf_scr[h] * cd_scr[0, h, 0:1, :] + _dot_tn((kf * kdf_scr[h]).astype(BF16), v)
                mu = jnp.mean(y, axis=-1, keepdims=True)
                d = y - mu
                var = jnp.mean(d * d, axis=-1, keepdims=True)
                yn = d * lax.rsqrt(var + GN_EPS)
                o_ref[rows, sl] = (yn * gain_ref[:, sl] * g_ref[rows, sl].astype(F32)).astype(o_ref.dtype)


def _retention(decay_logit, q, k, v, g, gain, c, rows):
    s, w = q.shape
    nc, nb = s // c, s // rows
    assert rows % c == 0
    fwd_only = lambda ph, n: (ph * n, 0)
    both = lambda ph, n: (ph * n + (1 - ph) * (nb - 1 - n), 0)
    hd = RET_HEAD_DIM
    return pl.pallas_call(
        _retention_kernel,
        grid=(2, nb),
        in_specs=[
            pl.BlockSpec(memory_space=pltpu.SMEM),
            pl.BlockSpec((rows, w), fwd_only), pl.BlockSpec((rows, w), both), pl.BlockSpec((rows, w), both),
            pl.BlockSpec((rows, w), fwd_only), _const_spec((1, w)),
        ],
        out_specs=pl.BlockSpec((rows, w), fwd_only),
        out_shape=jax.ShapeDtypeStruct((s, w), BF16),
        scratch_shapes=[
            pltpu.VMEM((RET_HEADS, hd, hd), F32), pltpu.VMEM((RET_HEADS, hd, hd), F32),
            pltpu.VMEM((nc, RET_HEADS, hd, hd), BF16), pltpu.VMEM((RET_HEADS, c, c), F32),
            pltpu.VMEM((RET_HEADS, c, hd), F32), pltpu.VMEM((RET_HEADS, c, hd), F32),
            pltpu.VMEM((RET_HEADS, c, hd), F32), pltpu.VMEM((RET_HEADS, c, hd), F32),
            pltpu.VMEM((2, RET_HEADS, 8, hd), F32),
        ],
        compiler_params=pltpu.CompilerParams(dimension_semantics=("arbitrary", "arbitrary"),
                                             vmem_limit_bytes=VMEM_LIMIT),
        name="retention",
    )(decay_logit, q, k, v, g, gain)


def _stage_weights(pairs, stage, sem):
    slots, rows, cols = stage.shape
    chunks = [(src.at[pl.ds(r0, rows), pl.ds(c0, cols)], dst, r0, c0)
              for src, dst in pairs
              for c0 in range(0, src.shape[1], cols) for r0 in range(0, src.shape[0], rows)]
    copy = lambda i: pltpu.make_async_copy(chunks[i][0], stage.at[i % slots], sem.at[i % slots])
    for i in range(min(slots - 1, len(chunks))):
        copy(i).start()
    for i, (_, dst, r0, c0) in enumerate(chunks):
        if i + slots - 1 < len(chunks):
            copy(i + slots - 1).start()
        copy(i).wait()
        dst[r0:r0 + rows, c0:c0 + cols] = stage[i % slots].astype(BF16)


def _post_kernel(x_ref, o_ref, ry_ref, sga_ref, sgr_ref, p_ref, wao_hbm, wro_hbm, wout_hbm, mg_ref,
                 wup_hbm, wdn_hbm, pg_ref, wpg_hbm, wple_hbm, fg_ref, out_ref,
                 wao_ref, wro_ref, wout_ref, wup_ref, wdn_ref, wpg_ref, wple_ref, stage, sem):
    @pl.when(pl.program_id(0) == 0)
    def _load_weights():
        _stage_weights([(wao_hbm, wao_ref), (wro_hbm, wro_ref), (wout_hbm, wout_ref), (wup_hbm, wup_ref),
                        (wdn_hbm, wdn_ref), (wpg_hbm, wpg_ref), (wple_hbm, wple_ref)], stage, sem)

    d = x_ref.shape[1]
    merged = (sga_ref[...].astype(F32) * _dot(o_ref[...], wao_ref[...])
              + sgr_ref[...].astype(F32) * _dot(ry_ref[...], wro_ref[...]))
    x1 = x_ref[...] + _dot(merged.astype(BF16), wout_ref[...])
    hm = _rms(x1, mg_ref[...]).astype(BF16)
    x2 = x1
    for c in range(wup_ref.shape[1] // d):
        sl = slice(c * d, (c + 1) * d)
        u = jnp.maximum(_dot(hm, wup_ref[:, sl]), 0.0)
        x2 = x2 + _dot((u * u).astype(BF16), wdn_ref[sl, :])
    gate = _sigmoid(_dot(_rms(x2, pg_ref[...]).astype(BF16), wpg_ref[...]))
    x3 = x2 + gate * _dot(p_ref[...].astype(BF16), wple_ref[...])
    out_ref[...] = _rms(x3, fg_ref[...])


def _post(x, o, ry, sga, sgr, p, wao, wro, wout, mg, wup, wdn, pg, wpg, wple, fg, tm):
    s, d = x.shape
    row = lambda a: pl.BlockSpec((tm, a.shape[1]), lambda i: (i, 0))
    const = lambda a: _const_spec(a.shape)
    hbm = lambda a: pl.BlockSpec(memory_space=pl.ANY)
    weights = (wao, wro, wout, wup, wdn, wpg, wple)
    assert all(w.shape[0] % WEIGHT_CHUNK[0] == 0 and w.shape[1] % WEIGHT_CHUNK[1] == 0 for w in weights)
    return pl.pallas_call(
        _post_kernel, grid=(s // tm,),
        in_specs=[row(x), row(o), row(ry), row(sga), row(sgr), row(p), hbm(wao), hbm(wro), hbm(wout),
                  const(mg), hbm(wup), hbm(wdn), const(pg), hbm(wpg), hbm(wple), const(fg)],
        out_specs=pl.BlockSpec((tm, d), lambda i: (i, 0)),
        out_shape=jax.ShapeDtypeStruct((s, d), F32),
        scratch_shapes=[pltpu.VMEM(w.shape, BF16) for w in weights]
        + [pltpu.VMEM((WEIGHT_SLOTS,) + WEIGHT_CHUNK, F32), pltpu.SemaphoreType.DMA((WEIGHT_SLOTS,))],
        compiler_params=pltpu.CompilerParams(dimension_semantics=("arbitrary",), vmem_limit_bytes=VMEM_LIMIT),
        name="post",
    )(x, o, ry, sga, sgr, p, wao, wro, wout, mg, wup, wdn, pg, wpg, wple, fg)


def _rope_tables(seq_len, head_dim):
    n_axis = head_dim // 4
    lane = jnp.arange(LANES)
    freqs = ROPE_THETA ** (-(lane % n_axis).astype(F32) / n_axis)
    sign = jnp.where((lane % head_dim) < head_dim // 2, -1.0, 1.0).astype(F32)
    rows = jnp.arange(seq_len // GRID_W, dtype=F32)[:, None] * freqs
    cols = jnp.arange(GRID_W, dtype=F32)[:, None] * freqs
    return (jnp.cos(rows), jnp.cos(cols)), (jnp.sin(rows) * sign, jnp.sin(cols) * sign)


def _layer(x, p, mix_norm, w_in, attn_q_norm, attn_k_norm, ret_decay_logit, ret_norm_gain, w_attn_o, w_ret_o,
           w_out, mlp_norm, w_up, w_down, ple_norm, w_ple_gate, w_ple, out_gain, rope_a, rope_r, tk, tq, c, tm):
    reps = LANES // ATTN_HEAD_DIM
    lane = jnp.arange(LANES) // ATTN_HEAD_DIM
    gm = (lane[:, None] == lane[None, :]).astype(BF16)
    row2 = lambda a: a.reshape(1, -1)
    qT, qn2, k, kn2, vT, qr, kr, vr, gr, sga, sgr = _inproj(
        x, row2(mix_norm), w_in.astype(BF16), row2(jnp.tile(attn_q_norm, reps)), row2(jnp.tile(attn_k_norm, reps)),
        rope_a, rope_r, gm, tk)
    o = _attention(qT, qn2, k, vT, kn2, tq)
    ry = _retention(ret_decay_logit, qr, kr, vr, gr, row2(ret_norm_gain), c, min(RET_ROWS, x.shape[0]))
    return _post(x, o, ry, sga, sgr, p, w_attn_o, w_ret_o, w_out, row2(mlp_norm), w_up, w_down, row2(ple_norm),
                 w_ple_gate, w_ple, row2(out_gain), tm)


def kernel(x, p, mix_norm, w_in, attn_q_norm, attn_k_norm, ret_decay_logit, ret_norm_gain, w_attn_o, w_ret_o,
           w_out, mlp_norm, w_up, w_down, ple_norm, w_ple_gate, w_ple, final_norm):
    b, s, d = x.shape
    depth = p.shape[0]
    assert b == 1 and depth == 1, "single sequence, single layer"
    y = _layer(x[0], p[0, 0], mix_norm[0], w_in[0], attn_q_norm[0], attn_k_norm[0], ret_decay_logit[0],
               ret_norm_gain[0], w_attn_o[0], w_ret_o[0], w_out[0], mlp_norm[0], w_up[0], w_down[0], ple_norm[0],
               w_ple_gate[0], w_ple[0], final_norm, _rope_tables(s, ATTN_HEAD_DIM), _rope_tables(s, RET_HEAD_DIM),
               min(ATTN_K_TILE, s), min(ATTN_Q_TILE, s), min(RET_CHUNK, s), min(ROW_TILE, s))
    return y[None]
```
